```python
import math
import jax, jax.numpy as jnp
from jax import lax
import numpy as np

D_MODEL = 1024
BATCH = 8
SEQ = 4096
DEPTH = 4

N_MIXERS = 3
N_A = (DEPTH + 2) // 3
N_B = (DEPTH + 1) // 3
N_C = DEPTH // 3
N_META = 16
Q_BLOCK = 128
EPS = 1e-6

MLA_HEADS = 16
MLA_Q_RANK = 256
MLA_KV_RANK = 128
MLA_NOPE = 64
MLA_ROPE = 32
MLA_V = 64
ROPE_THETA = 10000.0

SC_WIDTH = 3

DIFF_HEADS = 8
DIFF_HEAD_DIM = D_MODEL // DIFF_HEADS // 2
LAMBDA_INIT_SCALE = 0.1

D_FF = 2816
FFN_CONV_WIDTH = 3

kernel_name = "hybrid_mla_shortconv_diffattn_trunk"


def rms_norm(x, g):
    xf = x.astype(jnp.float32)
    y = xf * lax.rsqrt(jnp.mean(xf * xf, axis=-1, keepdims=True) + EPS)
    return (y * g.astype(jnp.float32)).astype(x.dtype)


def causal_dwconv(h, w):
    K = w.shape[0]
    L = h.shape[1]
    hp = jnp.pad(h, ((0, 0), (K - 1, 0), (0, 0)))
    y = w[K - 1] * hp[:, K - 1:K - 1 + L]
    for j in range(K - 1):
        y = y + w[j] * hp[:, j:j + L]
    return y


def rope_tables(L, dtype):
    inv_freq = ROPE_THETA ** (-jnp.arange(0, MLA_ROPE, 2, dtype=jnp.float32) / MLA_ROPE)
    ang = jnp.arange(L, dtype=jnp.float32)[:, None] * inv_freq[None, :]
    return jnp.cos(ang).astype(dtype), jnp.sin(ang).astype(dtype)


def apply_rope(x, cos, sin):
    x1, x2 = jnp.split(x, 2, axis=-1)
    return jnp.concatenate([x1 * cos - x2 * sin, x2 * cos + x1 * sin], axis=-1)


def alibi_slopes(n_heads):
    return 2.0 ** (-8.0 * jnp.arange(1, n_heads + 1, dtype=jnp.float32) / n_heads)


def sweep_causal_queries(block_fn, q):
    B, L = q.shape[0], q.shape[1]
    n_blk = (L - N_META) // Q_BLOCK
    pos = jnp.arange(L, dtype=jnp.int32)
    out_meta = block_fn(q[:, :N_META], pos[:N_META])
    q_rest = jnp.moveaxis(q[:, N_META:].reshape(B, n_blk, Q_BLOCK, *q.shape[2:]), 1, 0)
    pos_rest = pos[N_META:].reshape(n_blk, Q_BLOCK)
    out_rest = lax.map(lambda a: block_fn(a[0], a[1]), (q_rest, pos_rest))
    out_rest = jnp.moveaxis(out_rest, 0, 1).reshape(B, L - N_META, *out_rest.shape[3:])
    return jnp.concatenate([out_meta, out_rest], axis=1)


def mla_mixer(h, w_in, g_q, g_kv, w_uq, w_ukv, w_o, cos, sin):
    B, L, _ = h.shape
    c = h @ w_in
    c_q, c_kv, k_r = jnp.split(c, [MLA_Q_RANK, MLA_Q_RANK + MLA_KV_RANK], axis=-1)
    q = (rms_norm(c_q, g_q) @ w_uq).reshape(B, L, MLA_HEADS, MLA_NOPE + MLA_ROPE)
    kv = (rms_norm(c_kv, g_kv) @ w_ukv).reshape(B, L, MLA_HEADS, MLA_NOPE + MLA_V)
    q_nope, q_rope = jnp.split(q, [MLA_NOPE], axis=-1)
    k_nope, v = jnp.split(kv, [MLA_NOPE], axis=-1)
    q = jnp.concatenate([q_nope, apply_rope(q_rope, cos[:, None, :], sin[:, None, :])], axis=-1)
    k_rope = apply_rope(k_r, cos, sin)
    k = jnp.concatenate(
        [k_nope, jnp.broadcast_to(k_rope[:, :, None, :], (B, L, MLA_HEADS, MLA_ROPE))], axis=-1)
    scale = (MLA_NOPE + MLA_ROPE) ** -0.5
    k_pos = jnp.arange(L, dtype=jnp.int32)

    def block(qb, q_pos):
        s = jnp.einsum('bqhd,bkhd->bhqk', qb, k).astype(jnp.float32) * scale
        s = jnp.where(k_pos[None, :] <= q_pos[:, None], s, -jnp.inf)
        p = jax.nn.softmax(s, axis=-1).astype(v.dtype)
        return jnp.einsum('bhqk,bkhd->bqhd', p, v)

    o = sweep_causal_queries(block, q)
    return o.reshape(B, L, MLA_HEADS * MLA_V) @ w_o


def short_conv_mixer(h, w_in, w_conv, w_out):
    gate_b, gate_c, u = jnp.split(h @ w_in, 3, axis=-1)
    return (gate_b * causal_dwconv(gate_c * u, w_conv)) @ w_out


def diff_attn_mixer(h, w_in, lq1, lk1, lq2, lk2, g_sub, w_o, lambda_init):
    B, L, _ = h.shape
    H, d = DIFF_HEADS, DIFF_HEAD_DIM
    q, k, v = jnp.split(h @ w_in, 3, axis=-1)
    q = q.reshape(B, L, H, 2, d)
    k = k.reshape(B, L, H, 2, d)
    v = v.reshape(B, L, H, 2 * d)
    f32 = jnp.float32
    lam = (jnp.exp(jnp.sum(lq1.astype(f32) * lk1.astype(f32)))
           - jnp.exp(jnp.sum(lq2.astype(f32) * lk2.astype(f32))) + lambda_init)
    slopes = alibi_slopes(H)
    k_pos = jnp.arange(L, dtype=jnp.int32)
    scale = d ** -0.5

    def block(qb, q_pos):
        s = jnp.einsum('bqhmd,bkhmd->bhmqk', qb, k).astype(f32) * scale
        dist = (q_pos[:, None] - k_pos[None, :]).astype(f32)
        s = s - slopes[None, :, None, None, None] * dist
        s = jnp.where(k_pos[None, :] <= q_pos[:, None], s, -jnp.inf)
        p = jax.nn.softmax(s, axis=-1)
        a = (p[:, :, 0] - lam * p[:, :, 1]).astype(v.dtype)
        return jnp.einsum('bhqk,bkhe->bqhe', a, v)

    o = sweep_causal_queries(block, q)
    o = rms_norm(o, g_sub) * (1.0 - lambda_init)
    return o.reshape(B, L, H * 2 * d) @ w_o


def conv_glu_ffn(h, w_up, w_conv, w_down):
    g, u = jnp.split(causal_dwconv(h @ w_up, w_conv), 2, axis=-1)
    return (jax.nn.silu(g) * u) @ w_down


def setup_inputs(seed: int = 0) -> dict:
    key = jax.random.key(seed)
    ks = jax.random.split(key, 23)
    f32 = jnp.float32

    def nrm(k, shape, scale):
        return jax.random.normal(k, shape, f32) * scale

    def gain(k, shape):
        return 1.0 + 0.1 * jax.random.normal(k, shape, f32)

    D, F = D_MODEL, D_FF
    Hd = DIFF_HEADS * 2 * DIFF_HEAD_DIM
    return {
        "x": nrm(ks[0], (BATCH, SEQ, D), 1.0),
        "meta_tokens": nrm(ks[1], (N_META, D), 1.0),
        "norms": gain(ks[2], (DEPTH, 4, D)),
        "mla_w_in": nrm(ks[3], (N_A, D, MLA_Q_RANK + MLA_KV_RANK + MLA_ROPE), D ** -0.5),
        "mla_norm_q": gain(ks[4], (N_A, MLA_Q_RANK)),
        "mla_norm_kv": gain(ks[5], (N_A, MLA_KV_RANK)),
        "mla_w_uq": nrm(ks[6], (N_A, MLA_Q_RANK, MLA_HEADS * (MLA_NOPE + MLA_ROPE)), MLA_Q_RANK ** -0.5),
        "mla_w_ukv": nrm(ks[7], (N_A, MLA_KV_RANK, MLA_HEADS * (MLA_NOPE + MLA_V)), MLA_KV_RANK ** -0.5),
        "mla_w_o": nrm(ks[8], (N_A, MLA_HEADS * MLA_V, D), (MLA_HEADS * MLA_V) ** -0.5),
        "sc_w_in": nrm(ks[9], (N_B, D, 3 * D), D ** -0.5),
        "sc_conv": nrm(ks[10], (N_B, SC_WIDTH, D), SC_WIDTH ** -0.5),
        "sc_w_out": nrm(ks[11], (N_B, D, D), D ** -0.5),
        "diff_w_in": nrm(ks[12], (N_C, D, 3 * Hd), D ** -0.5),
        "diff_lambda_q1": nrm(ks[13], (N_C, DIFF_HEAD_DIM), LAMBDA_INIT_SCALE),
        "diff_lambda_k1": nrm(ks[14], (N_C, DIFF_HEAD_DIM), LAMBDA_INIT_SCALE),
        "diff_lambda_q2": nrm(ks[15], (N_C, DIFF_HEAD_DIM), LAMBDA_INIT_SCALE),
        "diff_lambda_k2": nrm(ks[16], (N_C, DIFF_HEAD_DIM), LAMBDA_INIT_SCALE),
        "diff_subln": gain(ks[17], (N_C, 2 * DIFF_HEAD_DIM)),
        "diff_w_o": nrm(ks[18], (N_C, Hd, D), Hd ** -0.5),
        "ffn_w_up": nrm(ks[19], (DEPTH, D, 2 * F), D ** -0.5),
        "ffn_conv": nrm(ks[20], (DEPTH, FFN_CONV_WIDTH, 2 * F), FFN_CONV_WIDTH ** -0.5),
        "ffn_w_down": nrm(ks[21], (DEPTH, F, D), F ** -0.5),
    }


def reference(x, meta_tokens, norms, mla_w_in, mla_norm_q, mla_norm_kv, mla_w_uq, mla_w_ukv,
              mla_w_o, sc_w_in, sc_conv, sc_w_out, diff_w_in, diff_lambda_q1, diff_lambda_k1,
              diff_lambda_q2, diff_lambda_k2, diff_subln, diff_w_o, ffn_w_up, ffn_conv, ffn_w_down):
    B = x.shape[0]
    meta = jnp.broadcast_to(meta_tokens[None].astype(x.dtype), (B, N_META, D_MODEL))
    h = jnp.concatenate([meta, x], axis=1)
    L = h.shape[1]
    cos, sin = rope_tables(L, x.dtype)
    for i in range(DEPTH):
        kind, j = i % N_MIXERS, i // N_MIXERS
        hn = rms_norm(h, norms[i, 0])
        if kind == 0:
            m = mla_mixer(hn, mla_w_in[j], mla_norm_q[j], mla_norm_kv[j], mla_w_uq[j],
                          mla_w_ukv[j], mla_w_o[j], cos, sin)
        elif kind == 1:
            m = short_conv_mixer(hn, sc_w_in[j], sc_conv[j], sc_w_out[j])
        else:
            lambda_init = 0.8 - 0.6 * math.exp(-0.3 * i)
            m = diff_attn_mixer(hn, diff_w_in[j], diff_lambda_q1[j], diff_lambda_k1[j],
                                diff_lambda_q2[j], diff_lambda_k2[j], diff_subln[j],
                                diff_w_o[j], lambda_init)
        h = h + rms_norm(m, norms[i, 1])
        f = conv_glu_ffn(rms_norm(h, norms[i, 2]), ffn_w_up[i], ffn_conv[i], ffn_w_down[i])
        h = h + rms_norm(f, norms[i, 3])
    return h[:, N_META:]
```

```python
import functools
import math

import jax
import jax.numpy as jnp
from jax import lax
from jax.experimental import pallas as pl
from jax.experimental.pallas import tpu as pltpu

F32 = jnp.float32
BF16 = jnp.bfloat16

D_MODEL = 1024
N_META = 16
EPS = 1e-6
N_MIXERS = 3

MLA_HEADS = 16
MLA_Q_RANK = 256
MLA_KV_RANK = 128
MLA_NOPE = 64
MLA_ROPE = 32
MLA_V = 64
MLA_QK = MLA_NOPE + MLA_ROPE
ROPE_THETA = 10000.0
HEAD_LANES = 128

DIFF_HEADS = 8
DIFF_HEAD_DIM = 64

D_FF = 2816
FFN_CHUNK = 256
CONV_WIDTH = 3
HALO = 8

ATTN_TILE = 256
TOKEN_TILE_TARGET = 544
NEG_BIG = -1e30
VMEM_LIMIT = 52 * 1024 * 1024


def _rms(x, g):
    return x * lax.rsqrt(jnp.mean(x * x, axis=-1, keepdims=True) + EPS) * g


def _token_tile(lp):
    best = 16
    for t in range(16, min(lp, TOKEN_TILE_TARGET) + 1, 16):
        if lp % t == 0:
            best = t
    return best


def _const_spec(shape):
    nd = len(shape)
    return pl.BlockSpec(shape, lambda *_: (0,) * nd, pipeline_mode=pl.Buffered(1))


def _params(sem):
    return pltpu.CompilerParams(dimension_semantics=sem, vmem_limit_bytes=VMEM_LIMIT)


def _causal_conv(xs_ref, cur, cw, tm):
    return (cw[2:3] * cur + cw[1:2] * xs_ref[pl.ds(HALO - 1, tm), :]
            + cw[0:1] * xs_ref[pl.ds(HALO - 2, tm), :])


def _mla_pre_kernel(h_ref, g0_ref, win_ref, gq_ref, gkv_ref, wqa_ref, wqb_ref, wk_ref, wv_ref,
                    cq_ref, sq_ref, ck_ref, sk_ref, q_ref, k_ref, v_ref):
    hn = _rms(h_ref[0], g0_ref[...]).astype(BF16)
    c = jnp.dot(hn, win_ref[...], preferred_element_type=F32)
    cq = _rms(c[:, :MLA_Q_RANK], gq_ref[...]).astype(BF16)
    ckv = _rms(c[:, MLA_Q_RANK:MLA_Q_RANK + MLA_KV_RANK], gkv_ref[...]).astype(BF16)
    o = MLA_Q_RANK + MLA_KV_RANK
    k_rope = (c[:, o:o + HEAD_LANES] * ck_ref[...]
              + c[:, o + HEAD_LANES:o + 2 * HEAD_LANES] * sk_ref[...])
    cq_t = cq_ref[...]
    sq_t = sq_ref[...]
    for p in range(MLA_HEADS // 2):
        sl = slice(p * 2 * HEAD_LANES, (p + 1) * 2 * HEAD_LANES)
        qa = jnp.dot(cq, wqa_ref[:, sl], preferred_element_type=F32)
        qb = jnp.dot(cq, wqb_ref[:, sl], preferred_element_type=F32)
        ka = jnp.dot(ckv, wk_ref[:, sl], preferred_element_type=F32)
        for e in range(2):
            a = slice(e * HEAD_LANES, (e + 1) * HEAD_LANES)
            d = slice((2 * p + e) * HEAD_LANES, (2 * p + e + 1) * HEAD_LANES)
            q_ref[0, :, d] = (qa[:, a] * cq_t + qb[:, a] * sq_t).astype(BF16)
            k_ref[0, :, d] = (ka[:, a] + k_rope).astype(BF16)
    v_ref[0] = jnp.dot(ckv, wv_ref[...], preferred_element_type=F32).astype(BF16)


def _mla_pre(h, g0, w, tabs, tm):
    b, lp, d = h.shape
    hq = MLA_HEADS * HEAD_LANES
    hv = MLA_HEADS * MLA_V
    tok = lambda n: pl.BlockSpec((1, tm, n), lambda bi, i: (bi, i, 0))
    tab = pl.BlockSpec((tm, HEAD_LANES), lambda bi, i: (i, 0))
    return pl.pallas_call(
        _mla_pre_kernel,
        grid=(b, lp // tm),
        in_specs=[tok(d), _const_spec(g0.shape), _const_spec(w["win"].shape),
                  _const_spec(w["gq"].shape), _const_spec(w["gkv"].shape),
                  _const_spec(w["wqa"].shape), _const_spec(w["wqb"].shape),
                  _const_spec(w["wk"].shape), _const_spec(w["wv"].shape), tab, tab, tab, tab],
        out_specs=[tok(hq), tok(hq), tok(hv)],
        out_shape=[jax.ShapeDtypeStruct((b, lp, hq), BF16), jax.ShapeDtypeStruct((b, lp, hq), BF16),
                   jax.ShapeDtypeStruct((b, lp, hv), BF16)],
        compiler_params=_params(("parallel", "parallel")),
        name="mla_pre",
    )(h, g0, w["win"], w["gq"], w["gkv"], w["wqa"], w["wqb"], w["wk"], w["wv"], *tabs)


def _mla_attn_kernel(q_ref, k_ref, v_ref, o_ref, m_ref, l_ref, acc_ref):
    t = ATTN_TILE
    i = pl.program_id(2)
    low = lax.broadcasted_iota(jnp.int32, (t, HEAD_LANES), 1) < MLA_V
    m_ref[...] = jnp.full(m_ref.shape, NEG_BIG, F32)
    l_ref[...] = jnp.zeros(l_ref.shape, F32)
    acc_ref[...] = jnp.zeros(acc_ref.shape, F32)

    def step(j, masked):
        start = pl.multiple_of(j * t, t)
        ks = k_ref[0, pl.ds(start, t), :]
        vs = v_ref[0, pl.ds(start, t), :]
        alphas, pvs = [], []
        for e in range(2):
            a = slice(e * HEAD_LANES, (e + 1) * HEAD_LANES)
            s = lax.dot_general(q_ref[0, :, a], ks[:, a], (((1,), (1,)), ((), ())),
                                preferred_element_type=F32)
            if masked:
                row = lax.broadcasted_iota(jnp.int32, (t, t), 0)
                col = lax.broadcasted_iota(jnp.int32, (t, t), 1)
                s = jnp.where(col <= row, s, NEG_BIG)
            m_old = m_ref[e]
            m_new = jnp.maximum(m_old, jnp.max(s, axis=-1, keepdims=True))
            alpha = jnp.exp(m_old - m_new)
            p = jnp.exp(s - m_new)
            l_ref[e] = alpha * l_ref[e] + jnp.sum(p, axis=-1, keepdims=True)
            m_ref[e] = m_new
            alphas.append(alpha)
            pvs.append(jnp.dot(p.astype(BF16), vs, preferred_element_type=F32))
        acc_ref[...] = (jnp.where(low, alphas[0], alphas[1]) * acc_ref[...]
                        + jnp.where(low, pvs[0], pvs[1]))

    def body(j, carry):
        step(j, False)
        return carry

    lax.fori_loop(0, i, body, 0)
    step(i, True)
    o_ref[0] = (acc_ref[...] / jnp.where(low, l_ref[0], l_ref[1])).astype(BF16)


def _mla_attn(q, k, v):
    b, lp, _ = q.shape
    t = ATTN_TILE
    pairs = MLA_HEADS // 2
    return pl.pallas_call(
        _mla_attn_kernel,
        grid=(b, pairs, lp // t),
        in_specs=[pl.BlockSpec((1, t, 2 * HEAD_LANES), lambda bi, p, i: (bi, i, p)),
                  pl.BlockSpec((1, lp, 2 * HEAD_LANES), lambda bi, p, i: (bi, 0, p)),
                  pl.BlockSpec((1, lp, HEAD_LANES), lambda bi, p, i: (bi, 0, p))],
        out_specs=pl.BlockSpec((1, t, HEAD_LANES), lambda bi, p, i: (bi, i, p)),
        out_shape=jax.ShapeDtypeStruct((b, lp, MLA_HEADS * MLA_V), BF16),
        scratch_shapes=[pltpu.VMEM((2, t, 1), F32), pltpu.VMEM((2, t, 1), F32),
                        pltpu.VMEM((t, HEAD_LANES), F32)],
        compiler_params=_params(("parallel", "parallel", "arbitrary")),
        name="mla_attn",
    )(q, k, v)


def _norm_proj_kernel(h_ref, g_ref, w_ref, *out_refs, scales):
    hn = _rms(h_ref[0], g_ref[...]).astype(BF16)
    n = out_refs[0].shape[-1]
    for idx, (o_ref, sc) in enumerate(zip(out_refs, scales)):
        y = jnp.dot(hn, w_ref[:, idx * n:(idx + 1) * n], preferred_element_type=F32)
        if sc != 1.0:
            y = y * sc
        o_ref[0] = y.astype(BF16)


def _norm_proj(h, g, w, scales, tm, name):
    b, lp, d = h.shape
    n = w.shape[1] // len(scales)
    tok = lambda c: pl.BlockSpec((1, tm, c), lambda bi, i: (bi, i, 0))
    return pl.pallas_call(
        functools.partial(_norm_proj_kernel, scales=scales),
        grid=(b, lp // tm),
        in_specs=[tok(d), _const_spec(g.shape), _const_spec(w.shape)],
        out_specs=[tok(n) for _ in scales],
        out_shape=[jax.ShapeDtypeStruct((b, lp, n), BF16) for _ in scales],
        compiler_params=_params(("parallel", "parallel")),
        name=name,
    )(h, g, w)


def _sc_pre_kernel(h_ref, g_ref, w_ref, cw_ref, o_ref, xs_ref, *, tm):
    d = D_MODEL

    @pl.when(pl.program_id(1) == 0)
    def _():
        xs_ref[pl.ds(tm, HALO), :] = jnp.zeros((HALO, d), F32)

    hn = _rms(h_ref[0], g_ref[...]).astype(BF16)
    gate_c = jnp.dot(hn, w_ref[:, d:2 * d], preferred_element_type=F32)
    u = jnp.dot(hn, w_ref[:, 2 * d:], preferred_element_type=F32)
    z = gate_c * u
    xs_ref[pl.ds(0, HALO), :] = xs_ref[pl.ds(tm, HALO), :]
    xs_ref[pl.ds(HALO, tm), :] = z
    y = _causal_conv(xs_ref, z, cw_ref[...], tm)
    gate_b = jnp.dot(hn, w_ref[:, :d], preferred_element_type=F32)
    o_ref[0] = (gate_b * y).astype(BF16)


def _sc_pre(h, g, w, cw, tm):
    b, lp, d = h.shape
    tok = pl.BlockSpec((1, tm, d), lambda bi, i: (bi, i, 0))
    return pl.pallas_call(
        functools.partial(_sc_pre_kernel, tm=tm),
        grid=(b, lp // tm),
        in_specs=[tok, _const_spec(g.shape), _const_spec(w.shape), _const_spec(cw.shape)],
        out_specs=tok,
        out_shape=jax.ShapeDtypeStruct((b, lp, d), BF16),
        scratch_shapes=[pltpu.VMEM((tm + HALO, d), F32)],
        compiler_params=_params(("parallel", "arbitrary")),
        name="sc_pre",
    )(h, g, w, cw)


def _diff_attn_kernel(lam_ref, slope_ref, gsub_ref, q_ref, k_ref, v_ref, o_ref,
                      qs_ref, bias_ref, m_ref, l_ref, acc_ref, *, lambda_init):
    t = ATTN_TILE
    i = pl.program_id(2)
    slope = slope_ref[0][:, 0:1]
    low = lax.broadcasted_iota(jnp.int32, (t, HEAD_LANES), 1) < DIFF_HEAD_DIM
    q = q_ref[0]
    zero = jnp.zeros_like(q)
    qs_ref[pl.ds(0, t), :] = jnp.where(low, q, zero)
    qs_ref[pl.ds(t, t), :] = jnp.where(low, zero, q)
    row = lax.broadcasted_iota(jnp.int32, (t, t), 0)
    col = lax.broadcasted_iota(jnp.int32, (t, t), 1)
    bias_ref[...] = (col - row).astype(F32) * slope
    m_ref[...] = jnp.full(m_ref.shape, NEG_BIG, F32)
    l_ref[...] = jnp.zeros(l_ref.shape, F32)
    acc_ref[...] = jnp.zeros(acc_ref.shape, F32)

    def step(j, masked):
        start = pl.multiple_of(j * t, t)
        s_all = lax.dot_general(qs_ref[...], k_ref[0, pl.ds(start, t), :], (((1,), (1,)), ((), ())),
                                preferred_element_type=F32)
        vs = v_ref[0, pl.ds(start, t), :]
        tile_bias = slope * ((j - i) * t).astype(F32)
        for e in range(2):
            rows = pl.ds(e * t, t)
            s = s_all[e * t:(e + 1) * t] + bias_ref[...]
            if masked:
                s = jnp.where(col <= row, s, NEG_BIG)
            m_old = m_ref[rows, :]
            m_new = jnp.maximum(m_old, jnp.max(s, axis=-1, keepdims=True) + tile_bias)
            alpha = jnp.exp(m_old - m_new)
            p = jnp.exp(s - (m_new - tile_bias))
            l_ref[rows, :] = alpha * l_ref[rows, :] + jnp.sum(p, axis=-1, keepdims=True)
            m_ref[rows, :] = m_new
            acc_ref[rows, :] = alpha * acc_ref[rows, :] + jnp.dot(
                p.astype(BF16), vs, preferred_element_type=F32)

    def body(j, carry):
        step(j, False)
        return carry

    lax.fori_loop(0, i, body, 0)
    step(i, True)

    lv = lam_ref[...]
    lam = (jnp.exp(jnp.sum(lv[0:1] * lv[1:2], axis=-1, keepdims=True))
           - jnp.exp(jnp.sum(lv[2:3] * lv[3:4], axis=-1, keepdims=True)) + lambda_init)
    o1 = acc_ref[pl.ds(0, t), :] / l_ref[pl.ds(0, t), :]
    o2 = acc_ref[pl.ds(t, t), :] / l_ref[pl.ds(t, t), :]
    o = _rms(o1 - lam * o2, gsub_ref[...]) * (1.0 - lambda_init)
    o_ref[0] = o.astype(BF16)


def _diff_attn(q, k, v, lam_vecs, slopes, gsub, lambda_init):
    b, lp, _ = q.shape
    t = ATTN_TILE
    return pl.pallas_call(
        functools.partial(_diff_attn_kernel, lambda_init=lambda_init),
        grid=(b, DIFF_HEADS, lp // t),
        in_specs=[_const_spec(lam_vecs.shape),
                  pl.BlockSpec((1, 1, HEAD_LANES), lambda bi, h, i: (h, 0, 0)),
                  _const_spec(gsub.shape),
                  pl.BlockSpec((1, t, HEAD_LANES), lambda bi, h, i: (bi, i, h)),
                  pl.BlockSpec((1, lp, HEAD_LANES), lambda bi, h, i: (bi, 0, h)),
                  pl.BlockSpec((1, lp, HEAD_LANES), lambda bi, h, i: (bi, 0, h))],
        out_specs=pl.BlockSpec((1, t, HEAD_LANES), lambda bi, h, i: (bi, i, h)),
        out_shape=jax.ShapeDtypeStruct((b, lp, DIFF_HEADS * HEAD_LANES), BF16),
        scratch_shapes=[pltpu.VMEM((2 * t, HEAD_LANES), BF16), pltpu.VMEM((t, t), F32),
                        pltpu.VMEM((2 * t, 1), F32), pltpu.VMEM((2 * t, 1), F32),
                        pltpu.VMEM((2 * t, HEAD_LANES), F32)],
        compiler_params=_params(("parallel", "parallel", "arbitrary")),
        name="diff_attn",
    )(lam_vecs, slopes, gsub, q, k, v)


def _post_kernel(h_ref, o_ref, wo_ref, g_ref, wup_ref, cw_ref, wdn_ref, out_ref,
                 hn_ref, xs_ref, carry_ref, f_ref, *, tm):
    fc = FFN_CHUNK

    @pl.when(pl.program_id(1) == 0)
    def _():
        carry_ref[...] = jnp.zeros(carry_ref.shape, F32)

    g = g_ref[...]
    m = jnp.dot(o_ref[0], wo_ref[...], preferred_element_type=F32)
    h1 = h_ref[0] + _rms(m, g[1:2])
    out_ref[0] = h1
    hn_ref[...] = _rms(h1, g[2:3]).astype(BF16)
    f_ref[...] = jnp.zeros(f_ref.shape, F32)

    def chunk(c, carry):
        up = jnp.dot(hn_ref[...], wup_ref[c], preferred_element_type=F32)
        xs_ref[pl.ds(0, HALO), :] = carry_ref[c]
        xs_ref[pl.ds(HALO, tm), :] = up
        carry_ref[c] = up[tm - HALO:, :]
        y = _causal_conv(xs_ref, up, cw_ref[c], tm)
        gate = y[:, :fc]
        act = (gate * jax.nn.sigmoid(gate) * y[:, fc:]).astype(BF16)
        f_ref[...] += jnp.dot(act, wdn_ref[c], preferred_element_type=F32)
        return carry

    lax.fori_loop(0, D_FF // fc, chunk, 0)
    out_ref[0] = out_ref[0] + _rms(f_ref[...], g[3:4])


def _post(h, o, wo, g, wup, cw, wdn, tm):
    b, lp, d = h.shape
    nch = D_FF // FFN_CHUNK
    tok = pl.BlockSpec((1, tm, d), lambda bi, i: (bi, i, 0))
    return pl.pallas_call(
        functools.partial(_post_kernel, tm=tm),
        grid=(b, lp // tm),
        in_specs=[tok, tok, _const_spec(wo.shape), _const_spec(g.shape), _const_spec(wup.shape),
                  _const_spec(cw.shape), _const_spec(wdn.shape)],
        out_specs=tok,
        out_shape=jax.ShapeDtypeStruct((b, lp, d), F32),
        scratch_shapes=[pltpu.VMEM((tm, d), BF16), pltpu.VMEM((tm + HALO, 2 * FFN_CHUNK), F32),
                        pltpu.VMEM((nch, HALO, 2 * FFN_CHUNK), F32), pltpu.VMEM((tm, d), F32)],
        compiler_params=_params(("parallel", "arbitrary")),
        name="post_ffn",
    )(h, o, wo, g, wup, cw, wdn)


def _mla_weights(w_in, g_q, g_kv, w_uq, w_ukv):
    d = w_in.shape[0]
    qr, kvr, r2 = MLA_Q_RANK, MLA_KV_RANK, MLA_ROPE // 2
    kr = w_in[:, qr + kvr:]
    pad = lambda a, lo: jnp.pad(a, ((0, 0), (lo, HEAD_LANES - lo - a.shape[1])))
    kr_swapped = jnp.concatenate([kr[:, r2:], kr[:, :r2]], axis=1)
    win = jnp.concatenate([w_in[:, :qr + kvr], pad(kr, MLA_NOPE), pad(kr_swapped, MLA_NOPE)], axis=1)
    wq = w_uq.reshape(qr, MLA_HEADS, MLA_QK)
    wqa = jnp.pad(wq, ((0, 0), (0, 0), (0, HEAD_LANES - MLA_QK)))
    rope_swapped = jnp.concatenate([wq[..., MLA_NOPE + r2:], wq[..., MLA_NOPE:MLA_NOPE + r2]], axis=-1)
    wqb = jnp.pad(rope_swapped, ((0, 0), (0, 0), (MLA_NOPE, HEAD_LANES - MLA_QK)))
    wkv = w_ukv.reshape(kvr, MLA_HEADS, MLA_NOPE + MLA_V)
    wk = jnp.pad(wkv[..., :MLA_NOPE], ((0, 0), (0, 0), (0, HEAD_LANES - MLA_NOPE)))
    wv = wkv[..., MLA_NOPE:]
    flat = lambda a: a.reshape(a.shape[0], -1).astype(BF16)
    return dict(win=win.astype(BF16), gq=g_q[None], gkv=g_kv[None], wqa=flat(wqa), wqb=flat(wqb),
                wk=flat(wk), wv=flat(wv))


def _rope_tables(lp):
    inv_freq = ROPE_THETA ** (-jnp.arange(0, MLA_ROPE, 2, dtype=F32) / MLA_ROPE)
    ang = jnp.arange(lp, dtype=F32)[:, None] * inv_freq[None, :]
    cos, sin = jnp.cos(ang), jnp.sin(ang)
    lay = lambda nope, a, b_: jnp.concatenate(
        [jnp.full((lp, MLA_NOPE), nope, F32), a, b_, jnp.zeros((lp, HEAD_LANES - MLA_QK), F32)], axis=1)
    scale = MLA_QK ** -0.5
    return (lay(1.0, cos, cos) * scale, lay(0.0, -sin, sin) * scale, lay(0.0, cos, cos), lay(0.0, -sin, sin))


def _ffn_weights(w_up, w_conv, w_down):
    d, f, fc = w_up.shape[0], D_FF, FFN_CHUNK
    nch = f // fc
    chunked = lambda a: a.reshape(a.shape[0], 2, nch, fc).transpose(2, 0, 1, 3).reshape(nch, a.shape[0], 2 * fc)
    return chunked(w_up).astype(BF16), chunked(w_conv), w_down.reshape(nch, fc, d).astype(BF16)


def kernel(x, meta_tokens, norms, mla_w_in, mla_norm_q, mla_norm_kv, mla_w_uq, mla_w_ukv, mla_w_o, sc_w_in, sc_conv, sc_w_out, diff_w_in, diff_lambda_q1, diff_lambda_k1, diff_lambda_q2, diff_lambda_k2, diff_subln, diff_w_o, ffn_w_up, ffn_conv, ffn_w_down):
    b, seq, d = x.shape
    depth = norms.shape[0]
    length = N_META + seq
    lp = -(-length // ATTN_TILE) * ATTN_TILE
    tm = _token_tile(lp)
    meta = jnp.broadcast_to(meta_tokens[None].astype(x.dtype), (b, N_META, d))
    h = jnp.concatenate([meta, x, jnp.zeros((b, lp - length, d), x.dtype)], axis=1)
    tabs = _rope_tables(lp)
    slopes = 2.0 ** (-8.0 * jnp.arange(1, DIFF_HEADS + 1, dtype=F32) / DIFF_HEADS)
    slopes = jnp.broadcast_to(slopes[:, None, None], (DIFF_HEADS, 1, HEAD_LANES))

    for i in range(depth):
        kind, j = i % N_MIXERS, i // N_MIXERS
        g = norms[i]
        if kind == 0:
            w = _mla_weights(mla_w_in[j], mla_norm_q[j], mla_norm_kv[j], mla_w_uq[j], mla_w_ukv[j])
            q, k, v = _mla_pre(h, g[0:1], w, tabs, tm)
            o = _mla_attn(q, k, v)
            wo = mla_w_o[j]
        elif kind == 1:
            o = _sc_pre(h, g[0:1], sc_w_in[j].astype(BF16), sc_conv[j], tm)
            wo = sc_w_out[j]
        else:
            lambda_init = 0.8 - 0.6 * math.exp(-0.3 * i)
            q, k, v = _norm_proj(h, g[0:1], diff_w_in[j].astype(BF16),
                                 (DIFF_HEAD_DIM ** -0.5, 1.0, 1.0), tm, "diff_pre")
            lam_vecs = jnp.stack([diff_lambda_q1[j], diff_lambda_k1[j], diff_lambda_q2[j], diff_lambda_k2[j]])
            o = _diff_attn(q, k, v, lam_vecs, slopes, diff_subln[j][None], lambda_init)
            wo = diff_w_o[j]
        wup, cw, wdn = _ffn_weights(ffn_w_up[i], ffn_conv[i], ffn_w_down[i])
        h = _post(h, o, wo.astype(BF16), g, wup, cw, wdn, tm)
    return h[:, N_META:length]
```

```python
import functools
import math

import jax
import jax.numpy as jnp
from jax import lax
from jax.experimental import pallas as pl
from jax.experimental.pallas import tpu as pltpu

F32 = jnp.float32
BF16 = jnp.bfloat16

D_MODEL = 1024
N_META = 16
EPS = 1e-6
N_MIXERS = 3
LOG2E = math.log2(math.e)

MLA_HEADS = 16
MLA_Q_RANK = 256
MLA_KV_RANK = 128
MLA_NOPE = 64
MLA_ROPE = 32
MLA_V = 64
MLA_QK = MLA_NOPE + MLA_ROPE
ROPE_THETA = 10000.0
HEAD_LANES = 128
ONES_ROWS = 16
MLA_GROUP = 16
MLA_VT_ROWS = MLA_V + ONES_ROWS

DIFF_HEADS = 8
DIFF_HEAD_DIM = 64
DIFF_GROUP = 8
DIFF_VT_ROWS = 2 * DIFF_HEAD_DIM + ONES_ROWS

D_FF = 2816
FFN_CHUNK = 256
HALO = 8

ATTN_TILE = 256
LOGITS_AHEAD = 3
TOKEN_TILE_TARGET = 544
NEG_BIG = -1e30
VMEM_LIMIT = 52 * 1024 * 1024


def _rms(x, g):
    return x * lax.rsqrt(jnp.mean(x * x, axis=-1, keepdims=True) + EPS) * g


def _dot_nt(a, b):
    return lax.dot_general(a, b, (((1,), (1,)), ((), ())), preferred_element_type=F32)


def _token_tile(lp):
    best = 16
    for t in range(16, min(lp, TOKEN_TILE_TARGET) + 1, 16):
        if lp % t == 0:
            best = t
    return best


def _const_spec(shape):
    nd = len(shape)
    return pl.BlockSpec(shape, lambda *_: (0,) * nd, pipeline_mode=pl.Buffered(1))


def _params(sem):
    return pltpu.CompilerParams(dimension_semantics=sem, vmem_limit_bytes=VMEM_LIMIT)


def _causal_conv(xs_ref, cur, cw, tm):
    return (cw[2:3] * cur + cw[1:2] * xs_ref[pl.ds(HALO - 1, tm), :]
            + cw[0:1] * xs_ref[pl.ds(HALO - 2, tm), :])


def _store_vt(vt_ref, vt, heads, width):
    t = vt.shape[1]
    for hd in range(heads):
        vt_ref[0, hd, 0, pl.ds(0, width), :] = vt[hd * width:(hd + 1) * width].astype(BF16)
        vt_ref[0, hd, 0, pl.ds(width, ONES_ROWS), :] = jnp.ones((ONES_ROWS, t), BF16)


def _softmax_tile(s, m_ref, e, tile_bias):
    m_old = m_ref[e]
    m_blk = jnp.max(s, axis=0, keepdims=True)
    if tile_bias is not None:
        m_blk = m_blk + tile_bias
    m_new = jnp.maximum(m_old, m_blk)
    m_ref[e] = m_new
    shift = m_new if tile_bias is None else m_new - tile_bias
    return jnp.exp2(s - shift).astype(BF16), jnp.exp2(m_old - m_new)


def _accumulate(vt_blk, p, alpha, acc_ref, e):
    pv = jnp.dot(vt_blk, p, preferred_element_type=F32)
    acc_ref[e] = alpha * acc_ref[e] + pv


def _pipelined_heads(n, logits, softmax, accumulate):
    ahead = [logits(e) for e in range(min(LOGITS_AHEAD, n))]
    pending = None
    for e in range(n):
        s = ahead.pop(0)
        if e + LOGITS_AHEAD < n:
            ahead.append(logits(e + LOGITS_AHEAD))
        p, alpha = softmax(e, s)
        if pending is not None:
            accumulate(*pending)
        pending = (e, p, alpha)
    accumulate(*pending)


def _causal_tile_mask(t):
    key = lax.broadcasted_iota(jnp.int32, (t, t), 0)
    query = lax.broadcasted_iota(jnp.int32, (t, t), 1)
    return key <= query


def _mla_pre_kernel(h_ref, g0_ref, win_ref, gq_ref, gkv_ref, wqa_ref, wqb_ref, wk_ref, wv_ref,
                    cq_ref, sq_ref, ck_ref, sk_ref, qt_ref, k_ref, vt_ref):
    hn = _rms(h_ref[0], g0_ref[...]).astype(BF16)
    c = jnp.dot(hn, win_ref[...], preferred_element_type=F32)
    cq = _rms(c[:, :MLA_Q_RANK], gq_ref[...]).astype(BF16)
    ckv = _rms(c[:, MLA_Q_RANK:MLA_Q_RANK + MLA_KV_RANK], gkv_ref[...]).astype(BF16)
    o = MLA_Q_RANK + MLA_KV_RANK
    k_rope = (c[:, o:o + HEAD_LANES] * ck_ref[...]
              + c[:, o + HEAD_LANES:o + 2 * HEAD_LANES] * sk_ref[...])
    ka = jnp.dot(ckv, wk_ref[...], preferred_element_type=F32)
    for hd in range(MLA_HEADS):
        d = slice(hd * HEAD_LANES, (hd + 1) * HEAD_LANES)
        k_ref[0, :, d] = (ka[:, d] + k_rope).astype(BF16)
    cq_t = cq_ref[...]
    sq_t = sq_ref[...]
    per = 4
    for c0 in range(0, MLA_HEADS, per):
        rows = slice(c0 * HEAD_LANES, (c0 + per) * HEAD_LANES)
        qa = _dot_nt(wqa_ref[rows, :], cq)
        qb = _dot_nt(wqb_ref[rows, :], cq)
        for e in range(per):
            a = slice(e * HEAD_LANES, (e + 1) * HEAD_LANES)
            qt_ref[0, c0 + e] = (qa[a] * cq_t + qb[a] * sq_t).astype(BF16)
    _store_vt(vt_ref, _dot_nt(wv_ref[...], ckv), MLA_HEADS, MLA_V)


def _mla_pre(h, g0, w, tabs):
    b, lp, d = h.shape
    t = ATTN_TILE
    nblk = lp // t
    tok = lambda n: pl.BlockSpec((1, t, n), lambda bi, i: (bi, i, 0))
    tab = pl.BlockSpec((t, HEAD_LANES), lambda bi, i: (i, 0))
    tab_t = pl.BlockSpec((HEAD_LANES, t), lambda bi, i: (0, i))
    return pl.pallas_call(
        _mla_pre_kernel,
        grid=(b, nblk),
        in_specs=[tok(d), _const_spec(g0.shape), _const_spec(w["win"].shape),
                  _const_spec(w["gq"].shape), _const_spec(w["gkv"].shape),
                  _const_spec(w["wqa"].shape), _const_spec(w["wqb"].shape),
                  _const_spec(w["wk"].shape), _const_spec(w["wv"].shape), tab_t, tab_t, tab, tab],
        out_specs=[pl.BlockSpec((1, MLA_HEADS, HEAD_LANES, t), lambda bi, i: (bi, 0, 0, i)),
                   tok(MLA_HEADS * HEAD_LANES),
                   pl.BlockSpec((1, MLA_HEADS, 1, MLA_VT_ROWS, t), lambda bi, i: (bi, 0, i, 0, 0))],
        out_shape=[jax.ShapeDtypeStruct((b, MLA_HEADS, HEAD_LANES, lp), BF16),
                   jax.ShapeDtypeStruct((b, lp, MLA_HEADS * HEAD_LANES), BF16),
                   jax.ShapeDtypeStruct((b, MLA_HEADS, nblk, MLA_VT_ROWS, t), BF16)],
        compiler_params=_params(("parallel", "parallel")),
        name="mla_pre",
    )(h, g0, w["win"], w["gq"], w["gkv"], w["wqa"], w["wqb"], w["wk"], w["wv"], *tabs)


def _mla_attn_kernel(qt_ref, k_ref, vt_ref, o_ref, m_ref, acc_ref):
    t = ATTN_TILE
    i = pl.program_id(2)
    m_ref[...] = jnp.full(m_ref.shape, NEG_BIG, F32)
    acc_ref[...] = jnp.zeros(acc_ref.shape, F32)

    def step(j, masked):
        start = pl.multiple_of(j * t, t)

        def logits(e):
            k_blk = k_ref[0, pl.ds(start, t), e * HEAD_LANES:(e + 1) * HEAD_LANES]
            s = jnp.dot(k_blk, qt_ref[0, e], preferred_element_type=F32)
            return jnp.where(_causal_tile_mask(t), s, NEG_BIG) if masked else s

        _pipelined_heads(
            MLA_GROUP, logits,
            lambda e, s: _softmax_tile(s, m_ref, e, None),
            lambda e, p, alpha: _accumulate(vt_ref[0, e, j], p, alpha, acc_ref, e))

    def body(j, carry):
        step(j, False)
        return carry

    lax.fori_loop(0, i, body, 0)
    step(i, True)
    for p in range(MLA_GROUP // 2):
        halves = [acc_ref[2 * p + e, pl.ds(0, MLA_V), :] / acc_ref[2 * p + e, pl.ds(MLA_V, 1), :]
                  for e in range(2)]
        o_ref[0, :, p * HEAD_LANES:(p + 1) * HEAD_LANES] = jnp.concatenate(halves, axis=0).T.astype(BF16)


def _mla_attn(qt, k, vt):
    b, lp, _ = k.shape
    t = ATTN_TILE
    g = MLA_GROUP
    nblk = lp // t
    return pl.pallas_call(
        _mla_attn_kernel,
        grid=(b, MLA_HEADS // g, nblk),
        in_specs=[pl.BlockSpec((1, g, HEAD_LANES, t), lambda bi, gi, i: (bi, gi, 0, i)),
                  pl.BlockSpec((1, lp, g * HEAD_LANES), lambda bi, gi, i: (bi, 0, gi),
                               pipeline_mode=pl.Buffered(1)),
                  pl.BlockSpec((1, g, nblk, MLA_VT_ROWS, t), lambda bi, gi, i: (bi, gi, 0, 0, 0),
                               pipeline_mode=pl.Buffered(1))],
        out_specs=pl.BlockSpec((1, t, g * MLA_V), lambda bi, gi, i: (bi, i, gi)),
        out_shape=jax.ShapeDtypeStruct((b, lp, MLA_HEADS * MLA_V), BF16),
        scratch_shapes=[pltpu.VMEM((g, 1, t), F32), pltpu.VMEM((g, MLA_VT_ROWS, t), F32)],
        compiler_params=_params(("parallel", "parallel", "arbitrary")),
        name="mla_attn",
    )(qt, k, vt)


def _sc_pre_kernel(h_ref, g_ref, w_ref, cw_ref, o_ref, xs_ref, *, tm):
    d = D_MODEL

    @pl.when(pl.program_id(1) == 0)
    def _():
        xs_ref[pl.ds(tm, HALO), :] = jnp.zeros((HALO, d), F32)

    hn = _rms(h_ref[0], g_ref[...]).astype(BF16)
    gate_c = jnp.dot(hn, w_ref[:, d:2 * d], preferred_element_type=F32)
    u = jnp.dot(hn, w_ref[:, 2 * d:], preferred_element_type=F32)
    z = gate_c * u
    xs_ref[pl.ds(0, HALO), :] = xs_ref[pl.ds(tm, HALO), :]
    xs_ref[pl.ds(HALO, tm), :] = z
    y = _causal_conv(xs_ref, z, cw_ref[...], tm)
    gate_b = jnp.dot(hn, w_ref[:, :d], preferred_element_type=F32)
    o_ref[0] = (gate_b * y).astype(BF16)


def _sc_pre(h, g, w, cw, tm):
    b, lp, d = h.shape
    tok = pl.BlockSpec((1, tm, d), lambda bi, i: (bi, i, 0))
    return pl.pallas_call(
        functools.partial(_sc_pre_kernel, tm=tm),
        grid=(b, lp // tm),
        in_specs=[tok, _const_spec(g.shape), _const_spec(w.shape), _const_spec(cw.shape)],
        out_specs=tok,
        out_shape=jax.ShapeDtypeStruct((b, lp, d), BF16),
        scratch_shapes=[pltpu.VMEM((tm + HALO, d), F32)],
        compiler_params=_params(("parallel", "arbitrary")),
        name="sc_pre",
    )(h, g, w, cw)


def _diff_pre_kernel(h_ref, g_ref, wqt_ref, wk_ref, wvt_ref, qt_ref, k_ref, vt_ref):
    hn = _rms(h_ref[0], g_ref[...]).astype(BF16)
    k_ref[0] = jnp.dot(hn, wk_ref[...], preferred_element_type=F32).astype(BF16)
    qt = _dot_nt(wqt_ref[...], hn) * (DIFF_HEAD_DIM ** -0.5 * LOG2E)
    for hd in range(DIFF_HEADS):
        qt_ref[0, hd] = qt[hd * HEAD_LANES:(hd + 1) * HEAD_LANES].astype(BF16)
    _store_vt(vt_ref, _dot_nt(wvt_ref[...], hn), DIFF_HEADS, 2 * DIFF_HEAD_DIM)


def _diff_pre(h, g, wqt, wk, wvt):
    b, lp, d = h.shape
    t = ATTN_TILE
    nblk = lp // t
    tok = pl.BlockSpec((1, t, d), lambda bi, i: (bi, i, 0))
    return pl.pallas_call(
        _diff_pre_kernel,
        grid=(b, nblk),
        in_specs=[tok, _const_spec(g.shape), _const_spec(wqt.shape), _const_spec(wk.shape),
                  _const_spec(wvt.shape)],
        out_specs=[pl.BlockSpec((1, DIFF_HEADS, HEAD_LANES, t), lambda bi, i: (bi, 0, 0, i)), tok,
                   pl.BlockSpec((1, DIFF_HEADS, 1, DIFF_VT_ROWS, t), lambda bi, i: (bi, 0, i, 0, 0))],
        out_shape=[jax.ShapeDtypeStruct((b, DIFF_HEADS, HEAD_LANES, lp), BF16),
                   jax.ShapeDtypeStruct((b, lp, d), BF16),
                   jax.ShapeDtypeStruct((b, DIFF_HEADS, nblk, DIFF_VT_ROWS, t), BF16)],
        compiler_params=_params(("parallel", "parallel")),
        name="diff_pre",
    )(h, g, wqt, wk, wvt)


def _diff_attn_kernel(lam_ref, slope_ref, gsub_ref, qt_ref, k_ref, vt_ref, o_ref,
                      qs_ref, bias_ref, m_ref, acc_ref, *, lambda_init):
    t = ATTN_TILE
    dv = 2 * DIFF_HEAD_DIM
    i = pl.program_id(2)
    first_map = lax.broadcasted_iota(jnp.int32, (HEAD_LANES, t), 0) < DIFF_HEAD_DIM
    key = lax.broadcasted_iota(jnp.int32, (t, t), 0)
    query = lax.broadcasted_iota(jnp.int32, (t, t), 1)
    in_tile_dist = (key - query).astype(F32)
    for e in range(DIFF_GROUP):
        q = qt_ref[0, e]
        zero = jnp.zeros_like(q)
        qs_ref[e, :, pl.ds(0, t)] = jnp.where(first_map, q, zero)
        qs_ref[e, :, pl.ds(t, t)] = jnp.where(first_map, zero, q)
        bias_ref[e] = in_tile_dist * slope_ref[e][:, 0:1]
    m_ref[...] = jnp.full(m_ref.shape, NEG_BIG, F32)
    acc_ref[...] = jnp.zeros(acc_ref.shape, F32)

    def step(j, masked):
        start = pl.multiple_of(j * t, t)
        tiles_apart = ((j - i) * t).astype(F32)

        def logits(e):
            k_blk = k_ref[0, pl.ds(start, t), e * HEAD_LANES:(e + 1) * HEAD_LANES]
            s = jnp.dot(k_blk, qs_ref[e], preferred_element_type=F32)
            bias = bias_ref[e]
            maps = [s[:, :t] + bias, s[:, t:] + bias]
            if masked:
                maps = [jnp.where(key <= query, sm, NEG_BIG) for sm in maps]
            return jnp.concatenate(maps, axis=1)

        def softmax(e, s):
            tile_bias = slope_ref[e][:, 0:1] * tiles_apart
            return _softmax_tile(s, m_ref, e, tile_bias)

        _pipelined_heads(
            DIFF_GROUP, logits, softmax,
            lambda e, p, alpha: _accumulate(vt_ref[0, e, j], p, alpha, acc_ref, e))

    def body(j, carry):
        step(j, False)
        return carry

    lax.fori_loop(0, i, body, 0)
    step(i, True)

    lv = lam_ref[...]
    lam = (jnp.exp(jnp.sum(lv[0:1] * lv[1:2], axis=-1, keepdims=True))
           - jnp.exp(jnp.sum(lv[2:3] * lv[3:4], axis=-1, keepdims=True)) + lambda_init)
    for e in range(DIFF_GROUP):
        o_t = acc_ref[e, pl.ds(0, dv), :] / acc_ref[e, pl.ds(dv, 1), :]
        o = (o_t[:, :t] - lam * o_t[:, t:]).T
        o_ref[0, :, e * dv:(e + 1) * dv] = (_rms(o, gsub_ref[...]) * (1.0 - lambda_init)).astype(BF16)


def _diff_attn(qt, k, vt, lam_vecs, slopes, gsub, lambda_init):
    b, lp, _ = k.shape
    t = ATTN_TILE
    g = DIFF_GROUP
    nblk = lp // t
    return pl.pallas_call(
        functools.partial(_diff_attn_kernel, lambda_init=lambda_init),
        grid=(b, DIFF_HEADS // g, nblk),
        in_specs=[_const_spec(lam_vecs.shape),
                  pl.BlockSpec((g, 1, HEAD_LANES), lambda bi, gi, i: (gi, 0, 0)),
                  _const_spec(gsub.shape),
                  pl.BlockSpec((1, g, HEAD_LANES, t), lambda bi, gi, i: (bi, gi, 0, i)),
                  pl.BlockSpec((1, lp, g * HEAD_LANES), lambda bi, gi, i: (bi, 0, gi),
                               pipeline_mode=pl.Buffered(1)),
                  pl.BlockSpec((1, g, nblk, DIFF_VT_ROWS, t), lambda bi, gi, i: (bi, gi, 0, 0, 0),
                               pipeline_mode=pl.Buffered(1))],
        out_specs=pl.BlockSpec((1, t, g * HEAD_LANES), lambda bi, gi, i: (bi, i, gi)),
        out_shape=jax.ShapeDtypeStruct((b, lp, DIFF_HEADS * HEAD_LANES), BF16),
        scratch_shapes=[pltpu.VMEM((g, HEAD_LANES, 2 * t), BF16), pltpu.VMEM((g, t, t), F32),
                        pltpu.VMEM((g, 1, 2 * t), F32), pltpu.VMEM((g, DIFF_VT_ROWS, 2 * t), F32)],
        compiler_params=_params(("parallel", "parallel", "arbitrary")),
        name="diff_attn",
    )(lam_vecs, slopes, gsub, qt, k, vt)


def _post_kernel(h_ref, o_ref, wo_ref, g_ref, wup_ref, cw_ref, wdn_ref, out_ref,
                 hn_ref, xs_ref, carry_ref, f_ref, *, tm):
    fc = FFN_CHUNK

    @pl.when(pl.program_id(1) == 0)
    def _():
        carry_ref[...] = jnp.zeros(carry_ref.shape, F32)

    g = g_ref[...]
    m = jnp.dot(o_ref[0], wo_ref[...], preferred_element_type=F32)
    h1 = h_ref[0] + _rms(m, g[1:2])
    out_ref[0] = h1
    hn_ref[...] = _rms(h1, g[2:3]).astype(BF16)
    f_ref[...] = jnp.zeros(f_ref.shape, F32)

    def up_proj(c):
        return jnp.dot(hn_ref[...], wup_ref[c], preferred_element_type=F32)

    def conv_act(c, up):
        xs = xs_ref.at[c % 2]
        xs[pl.ds(0, HALO), :] = carry_ref[c]
        xs[pl.ds(HALO, tm), :] = up
        carry_ref[c] = xs[pl.ds(tm, HALO), :]
        y = _causal_conv(xs, xs[pl.ds(HALO, tm), :], cw_ref[c], tm)
        gate = y[:, :fc]
        return (gate * jax.nn.sigmoid(gate) * y[:, fc:]).astype(BF16)

    nch = D_FF // fc
    up_next = up_proj(0)
    for c in range(nch):
        up = up_next
        if c + 1 < nch:
            up_next = up_proj(c + 1)
        act = conv_act(c, up)
        f_ref[...] += jnp.dot(act, wdn_ref[c], preferred_element_type=F32)
    out_ref[0] = out_ref[0] + _rms(f_ref[...], g[3:4])


def _post(h, o, wo, g, wup, cw, wdn, tm):
    b, lp, d = h.shape
    nch = D_FF // FFN_CHUNK
    tok = pl.BlockSpec((1, tm, d), lambda bi, i: (bi, i, 0))
    return pl.pallas_call(
        functools.partial(_post_kernel, tm=tm),
        grid=(b, lp // tm),
        in_specs=[tok, tok, _const_spec(wo.shape), _const_spec(g.shape), _const_spec(wup.shape),
                  _const_spec(cw.shape), _const_spec(wdn.shape)],
        out_specs=tok,
        out_shape=jax.ShapeDtypeStruct((b, lp, d), F32),
        scratch_shapes=[pltpu.VMEM((tm, d), BF16), pltpu.VMEM((2, tm + HALO, 2 * FFN_CHUNK), F32),
                        pltpu.VMEM((nch, HALO, 2 * FFN_CHUNK), F32), pltpu.VMEM((tm, d), F32)],
        compiler_params=_params(("parallel", "arbitrary")),
        name="post_ffn",
    )(h, o, wo, g, wup, cw, wdn)


def _mla_weights(w_in, g_q, g_kv, w_uq, w_ukv):
    qr, kvr, r2 = MLA_Q_RANK, MLA_KV_RANK, MLA_ROPE // 2
    kr = w_in[:, qr + kvr:]
    pad = lambda a, lo: jnp.pad(a, ((0, 0), (lo, HEAD_LANES - lo - a.shape[1])))
    kr_swapped = jnp.concatenate([kr[:, r2:], kr[:, :r2]], axis=1)
    win = jnp.concatenate([w_in[:, :qr + kvr], pad(kr, MLA_NOPE), pad(kr_swapped, MLA_NOPE)], axis=1)
    wq = w_uq.reshape(qr, MLA_HEADS, MLA_QK)
    wqa = jnp.pad(wq, ((0, 0), (0, 0), (0, HEAD_LANES - MLA_QK)))
    rope_swapped = jnp.concatenate([wq[..., MLA_NOPE + r2:], wq[..., MLA_NOPE:MLA_NOPE + r2]], axis=-1)
    wqb = jnp.pad(rope_swapped, ((0, 0), (0, 0), (MLA_NOPE, HEAD_LANES - MLA_QK)))
    wkv = w_ukv.reshape(kvr, MLA_HEADS, MLA_NOPE + MLA_V)
    wk = jnp.pad(wkv[..., :MLA_NOPE], ((0, 0), (0, 0), (0, HEAD_LANES - MLA_NOPE)))
    wv = wkv[..., MLA_NOPE:]
    flat = lambda a: a.reshape(a.shape[0], -1).astype(BF16)
    return dict(win=win.astype(BF16), gq=g_q[None], gkv=g_kv[None], wqa=flat(wqa).T, wqb=flat(wqb).T,
                wk=flat(wk), wv=flat(wv).T)


def _rope_tables(lp):
    inv_freq = ROPE_THETA ** (-jnp.arange(0, MLA_ROPE, 2, dtype=F32) / MLA_ROPE)
    ang = jnp.arange(lp, dtype=F32)[:, None] * inv_freq[None, :]
    cos, sin = jnp.cos(ang), jnp.sin(ang)
    lay = lambda nope, a, b_: jnp.concatenate(
        [jnp.full((lp, MLA_NOPE), nope, F32), a, b_, jnp.zeros((lp, HEAD_LANES - MLA_QK), F32)], axis=1)
    scale = MLA_QK ** -0.5 * LOG2E
    return ((lay(1.0, cos, cos) * scale).T, (lay(0.0, -sin, sin) * scale).T,
            lay(0.0, cos, cos), lay(0.0, -sin, sin))


def _ffn_weights(w_up, w_conv, w_down):
    d, f, fc = w_up.shape[0], D_FF, FFN_CHUNK
    nch = f // fc
    chunked = lambda a: a.reshape(a.shape[0], 2, nch, fc).transpose(2, 0, 1, 3).reshape(nch, a.shape[0], 2 * fc)
    return chunked(w_up).astype(BF16), chunked(w_conv), w_down.reshape(nch, fc, d).astype(BF16)


def kernel(x, meta_tokens, norms, mla_w_in, mla_norm_q, mla_norm_kv, mla_w_uq, mla_w_ukv, mla_w_o, sc_w_in, sc_conv, sc_w_out, diff_w_in, diff_lambda_q1, diff_lambda_k1, diff_lambda_q2, diff_lambda_k2, diff_subln, diff_w_o, ffn_w_up, ffn_conv, ffn_w_down):
    b, seq, d = x.shape
    depth = norms.shape[0]
    length = N_META + seq
    lp = -(-length // ATTN_TILE) * ATTN_TILE
    tm = _token_tile(lp)
    meta = jnp.broadcast_to(meta_tokens[None].astype(x.dtype), (b, N_META, d))
    h = jnp.concatenate([meta, x, jnp.zeros((b, lp - length, d), x.dtype)], axis=1)
    tabs = _rope_tables(lp)
    slopes = 2.0 ** (-8.0 * jnp.arange(1, DIFF_HEADS + 1, dtype=F32) / DIFF_HEADS) * LOG2E
    slopes = jnp.broadcast_to(slopes[:, None, None], (DIFF_HEADS, 1, HEAD_LANES))

    for i in range(depth):
        kind, j = i % N_MIXERS, i // N_MIXERS
        g = norms[i]
        if kind == 0:
            w = _mla_weights(mla_w_in[j], mla_norm_q[j], mla_norm_kv[j], mla_w_uq[j], mla_w_ukv[j])
            o = _mla_attn(*_mla_pre(h, g[0:1], w, tabs))
            wo = mla_w_o[j]
        elif kind == 1:
            o = _sc_pre(h, g[0:1], sc_w_in[j].astype(BF16), sc_conv[j], tm)
            wo = sc_w_out[j]
        else:
            lambda_init = 0.8 - 0.6 * math.exp(-0.3 * i)
            w_in = diff_w_in[j].astype(BF16)
            qt, k, vt = _diff_pre(h, g[0:1], w_in[:, :d].T, w_in[:, d:2 * d], w_in[:, 2 * d:].T)
            lam_vecs = jnp.stack([diff_lambda_q1[j], diff_lambda_k1[j], diff_lambda_q2[j], diff_lambda_k2[j]])
            o = _diff_attn(qt, k, vt, lam_vecs, slopes, diff_subln[j][None], lambda_init)
            wo = diff_w_o[j]
        wup, cw, wdn = _ffn_weights(ffn_w_up[i], ffn_conv[i], ffn_w_down[i])
        h = _post(h, o, wo.astype(BF16), g, wup, cw, wdn, tm)
    return h[:, N_META:length]
```

```python
import functools
import math

import jax
import jax.numpy as jnp
from jax import lax
from jax.experimental import pallas as pl
from jax.experimental.pallas import tpu as pltpu

F32 = jnp.float32
BF16 = jnp.bfloat16

D_MODEL = 1024
N_META = 16
EPS = 1e-6
N_MIXERS = 3
LOG2E = math.log2(math.e)

MLA_HEADS = 16
MLA_Q_RANK = 256
MLA_KV_RANK = 128
MLA_NOPE = 64
MLA_ROPE = 32
MLA_V = 64
MLA_QK = MLA_NOPE + MLA_ROPE
ROPE_THETA = 10000.0
HEAD_LANES = 128
ONES_ROWS = 16
MLA_GROUP = 16
MLA_VT_ROWS = MLA_V + ONES_ROWS

DIFF_HEADS = 8
DIFF_HEAD_DIM = 64
DIFF_GROUP = 8
DIFF_VT_ROWS = 2 * DIFF_HEAD_DIM + ONES_ROWS

D_FF = 2816
FFN_CHUNK = 256
HALO = 8

ATTN_TILE = 256
LOGITS_AHEAD = 3
ACCUMULATE_BEHIND = 2
TOKEN_TILE_TARGET = 544
NEG_BIG = -1e30
VMEM_LIMIT = 52 * 1024 * 1024


def _rms(x, g):
    return x * lax.rsqrt(jnp.mean(x * x, axis=-1, keepdims=True) + EPS) * g


def _dot_nt(a, b):
    return lax.dot_general(a, b, (((1,), (1,)), ((), ())), preferred_element_type=F32)


def _token_tile(lp):
    best = 16
    for t in range(16, min(lp, TOKEN_TILE_TARGET) + 1, 16):
        if lp % t == 0:
            best = t
    return best


def _const_spec(shape):
    nd = len(shape)
    return pl.BlockSpec(shape, lambda *_: (0,) * nd, pipeline_mode=pl.Buffered(1))


def _params(sem):
    return pltpu.CompilerParams(dimension_semantics=sem, vmem_limit_bytes=VMEM_LIMIT)


def _causal_conv(xs_ref, cur, cw, tm):
    return (cw[2:3] * cur + cw[1:2] * xs_ref[pl.ds(HALO - 1, tm), :]
            + cw[0:1] * xs_ref[pl.ds(HALO - 2, tm), :])


def _store_vt(vt_ref, vt, heads, width):
    t = vt.shape[1]
    for hd in range(heads):
        vt_ref[0, hd, 0, pl.ds(0, width), :] = vt[hd * width:(hd + 1) * width].astype(BF16)
        vt_ref[0, hd, 0, pl.ds(width, ONES_ROWS), :] = jnp.ones((ONES_ROWS, t), BF16)


def _softmax_tile(s, m_ref, e, tile_bias):
    m_old = m_ref[e]
    m_blk = jnp.max(s, axis=0, keepdims=True)
    if tile_bias is not None:
        m_blk = m_blk + tile_bias
    m_new = jnp.maximum(m_old, m_blk)
    m_ref[e] = m_new
    shift = m_new if tile_bias is None else m_new - tile_bias
    return jnp.exp2(s - shift).astype(BF16), jnp.exp2(m_old - m_new)


def _accumulate(vt_blk, p, alpha, acc_ref, e):
    pv = jnp.dot(vt_blk, p, preferred_element_type=F32)
    acc_ref[e] = alpha * acc_ref[e] + pv


def _pipelined_heads(n, logits, softmax, accumulate):
    ahead = [logits(e) for e in range(min(LOGITS_AHEAD, n))]
    pending = []
    for e in range(n):
        s = ahead.pop(0)
        if e + LOGITS_AHEAD < n:
            ahead.append(logits(e + LOGITS_AHEAD))
        p, alpha = softmax(e, s)
        pending.append((e, p, alpha))
        if len(pending) > ACCUMULATE_BEHIND:
            accumulate(*pending.pop(0))
    for item in pending:
        accumulate(*item)


def _causal_tile_mask(t):
    key = lax.broadcasted_iota(jnp.int32, (t, t), 0)
    query = lax.broadcasted_iota(jnp.int32, (t, t), 1)
    return key <= query


def _mla_pre_kernel(h_ref, g0_ref, win_ref, gq_ref, gkv_ref, wqa_ref, wqb_ref, wk_ref, wv_ref,
                    cq_ref, sq_ref, ck_ref, sk_ref, qt_ref, k_ref, vt_ref):
    hn = _rms(h_ref[0], g0_ref[...]).astype(BF16)
    c = jnp.dot(hn, win_ref[...], preferred_element_type=F32)
    cq = _rms(c[:, :MLA_Q_RANK], gq_ref[...]).astype(BF16)
    ckv = _rms(c[:, MLA_Q_RANK:MLA_Q_RANK + MLA_KV_RANK], gkv_ref[...]).astype(BF16)
    o = MLA_Q_RANK + MLA_KV_RANK
    k_rope = (c[:, o:o + HEAD_LANES] * ck_ref[...]
              + c[:, o + HEAD_LANES:o + 2 * HEAD_LANES] * sk_ref[...])
    ka = jnp.dot(ckv, wk_ref[...], preferred_element_type=F32)
    for hd in range(MLA_HEADS):
        d = slice(hd * HEAD_LANES, (hd + 1) * HEAD_LANES)
        k_ref[0, :, d] = (ka[:, d] + k_rope).astype(BF16)
    cq_t = cq_ref[...]
    sq_t = sq_ref[...]
    per = 4
    for c0 in range(0, MLA_HEADS, per):
        rows = slice(c0 * HEAD_LANES, (c0 + per) * HEAD_LANES)
        qa = _dot_nt(wqa_ref[rows, :], cq)
        qb = _dot_nt(wqb_ref[rows, :], cq)
        for e in range(per):
            a = slice(e * HEAD_LANES, (e + 1) * HEAD_LANES)
            qt_ref[0, c0 + e] = (qa[a] * cq_t + qb[a] * sq_t).astype(BF16)
    _store_vt(vt_ref, _dot_nt(wv_ref[...], ckv), MLA_HEADS, MLA_V)


def _mla_pre(h, g0, w, tabs):
    b, lp, d = h.shape
    t = ATTN_TILE
    nblk = lp // t
    tok = lambda n: pl.BlockSpec((1, t, n), lambda bi, i: (bi, i, 0))
    tab = pl.BlockSpec((t, HEAD_LANES), lambda bi, i: (i, 0))
    tab_t = pl.BlockSpec((HEAD_LANES, t), lambda bi, i: (0, i))
    return pl.pallas_call(
        _mla_pre_kernel,
        grid=(b, nblk),
        in_specs=[tok(d), _const_spec(g0.shape), _const_spec(w["win"].shape),
                  _const_spec(w["gq"].shape), _const_spec(w["gkv"].shape),
                  _const_spec(w["wqa"].shape), _const_spec(w["wqb"].shape),
                  _const_spec(w["wk"].shape), _const_spec(w["wv"].shape), tab_t, tab_t, tab, tab],
        out_specs=[pl.BlockSpec((1, MLA_HEADS, HEAD_LANES, t), lambda bi, i: (bi, 0, 0, i)),
                   tok(MLA_HEADS * HEAD_LANES),
                   pl.BlockSpec((1, MLA_HEADS, 1, MLA_VT_ROWS, t), lambda bi, i: (bi, 0, i, 0, 0))],
        out_shape=[jax.ShapeDtypeStruct((b, MLA_HEADS, HEAD_LANES, lp), BF16),
                   jax.ShapeDtypeStruct((b, lp, MLA_HEADS * HEAD_LANES), BF16),
                   jax.ShapeDtypeStruct((b, MLA_HEADS, nblk, MLA_VT_ROWS, t), BF16)],
        compiler_params=_params(("parallel", "parallel")),
        name="mla_pre",
    )(h, g0, w["win"], w["gq"], w["gkv"], w["wqa"], w["wqb"], w["wk"], w["wv"], *tabs)


def _mla_attn_kernel(qt_ref, k_ref, vt_ref, o_ref, m_ref, acc_ref):
    t = ATTN_TILE
    i = pl.program_id(2)
    m_ref[...] = jnp.full(m_ref.shape, NEG_BIG, F32)
    acc_ref[...] = jnp.zeros(acc_ref.shape, F32)

    def step(j, masked):
        start = pl.multiple_of(j * t, t)

        def logits(e):
            k_blk = k_ref[0, pl.ds(start, t), e * HEAD_LANES:(e + 1) * HEAD_LANES]
            s = jnp.dot(k_blk, qt_ref[0, e], preferred_element_type=F32)
            return jnp.where(_causal_tile_mask(t), s, NEG_BIG) if masked else s

        _pipelined_heads(
            MLA_GROUP, logits,
            lambda e, s: _softmax_tile(s, m_ref, e, None),
            lambda e, p, alpha: _accumulate(vt_ref[0, e, j], p, alpha, acc_ref, e))

    def body(j, carry):
        step(j, False)
        return carry

    lax.fori_loop(0, i, body, 0)
    step(i, True)
    for p in range(MLA_GROUP // 2):
        halves = [acc_ref[2 * p + e, pl.ds(0, MLA_V), :] / acc_ref[2 * p + e, pl.ds(MLA_V, 1), :]
                  for e in range(2)]
        o_ref[0, :, p * HEAD_LANES:(p + 1) * HEAD_LANES] = jnp.concatenate(halves, axis=0).T.astype(BF16)


def _mla_attn(qt, k, vt):
    b, lp, _ = k.shape
    t = ATTN_TILE
    g = MLA_GROUP
    nblk = lp // t
    return pl.pallas_call(
        _mla_attn_kernel,
        grid=(b, MLA_HEADS // g, nblk),
        in_specs=[pl.BlockSpec((1, g, HEAD_LANES, t), lambda bi, gi, i: (bi, gi, 0, i)),
                  pl.BlockSpec((1, lp, g * HEAD_LANES), lambda bi, gi, i: (bi, 0, gi),
                               pipeline_mode=pl.Buffered(1)),
                  pl.BlockSpec((1, g, nblk, MLA_VT_ROWS, t), lambda bi, gi, i: (bi, gi, 0, 0, 0),
                               pipeline_mode=pl.Buffered(1))],
        out_specs=pl.BlockSpec((1, t, g * MLA_V), lambda bi, gi, i: (bi, i, gi)),
        out_shape=jax.ShapeDtypeStruct((b, lp, MLA_HEADS * MLA_V), BF16),
        scratch_shapes=[pltpu.VMEM((g, 1, t), F32), pltpu.VMEM((g, MLA_VT_ROWS, t), F32)],
        compiler_params=_params(("parallel", "parallel", "arbitrary")),
        name="mla_attn",
    )(qt, k, vt)


def _sc_pre_kernel(h_ref, g_ref, w_ref, cw_ref, o_ref, xs_ref, *, tm):
    d = D_MODEL

    @pl.when(pl.program_id(1) == 0)
    def _():
        xs_ref[pl.ds(tm, HALO), :] = jnp.zeros((HALO, d), F32)

    hn = _rms(h_ref[0], g_ref[...]).astype(BF16)
    gate_c = jnp.dot(hn, w_ref[:, d:2 * d], preferred_element_type=F32)
    u = jnp.dot(hn, w_ref[:, 2 * d:], preferred_element_type=F32)
    z = gate_c * u
    xs_ref[pl.ds(0, HALO), :] = xs_ref[pl.ds(tm, HALO), :]
    xs_ref[pl.ds(HALO, tm), :] = z
    y = _causal_conv(xs_ref, z, cw_ref[...], tm)
    gate_b = jnp.dot(hn, w_ref[:, :d], preferred_element_type=F32)
    o_ref[0] = (gate_b * y).astype(BF16)


def _sc_pre(h, g, w, cw, tm):
    b, lp, d = h.shape
    tok = pl.BlockSpec((1, tm, d), lambda bi, i: (bi, i, 0))
    return pl.pallas_call(
        functools.partial(_sc_pre_kernel, tm=tm),
        grid=(b, lp // tm),
        in_specs=[tok, _const_spec(g.shape), _const_spec(w.shape), _const_spec(cw.shape)],
        out_specs=tok,
        out_shape=jax.ShapeDtypeStruct((b, lp, d), BF16),
        scratch_shapes=[pltpu.VMEM((tm + HALO, d), F32)],
        compiler_params=_params(("parallel", "arbitrary")),
        name="sc_pre",
    )(h, g, w, cw)


def _diff_pre_kernel(h_ref, g_ref, wqt_ref, wk_ref, wvt_ref, qt_ref, k_ref, vt_ref):
    hn = _rms(h_ref[0], g_ref[...]).astype(BF16)
    k_ref[0] = jnp.dot(hn, wk_ref[...], preferred_element_type=F32).astype(BF16)
    qt = _dot_nt(wqt_ref[...], hn) * (DIFF_HEAD_DIM ** -0.5 * LOG2E)
    for hd in range(DIFF_HEADS):
        qt_ref[0, hd] = qt[hd * HEAD_LANES:(hd + 1) * HEAD_LANES].astype(BF16)
    _store_vt(vt_ref, _dot_nt(wvt_ref[...], hn), DIFF_HEADS, 2 * DIFF_HEAD_DIM)


def _diff_pre(h, g, wqt, wk, wvt):
    b, lp, d = h.shape
    t = ATTN_TILE
    nblk = lp // t
    tok = pl.BlockSpec((1, t, d), lambda bi, i: (bi, i, 0))
    return pl.pallas_call(
        _diff_pre_kernel,
        grid=(b, nblk),
        in_specs=[tok, _const_spec(g.shape), _const_spec(wqt.shape), _const_spec(wk.shape),
                  _const_spec(wvt.shape)],
        out_specs=[pl.BlockSpec((1, DIFF_HEADS, HEAD_LANES, t), lambda bi, i: (bi, 0, 0, i)), tok,
                   pl.BlockSpec((1, DIFF_HEADS, 1, DIFF_VT_ROWS, t), lambda bi, i: (bi, 0, i, 0, 0))],
        out_shape=[jax.ShapeDtypeStruct((b, DIFF_HEADS, HEAD_LANES, lp), BF16),
                   jax.ShapeDtypeStruct((b, lp, d), BF16),
                   jax.ShapeDtypeStruct((b, DIFF_HEADS, nblk, DIFF_VT_ROWS, t), BF16)],
        compiler_params=_params(("parallel", "parallel")),
        name="diff_pre",
    )(h, g, wqt, wk, wvt)


def _diff_attn_kernel(lam_ref, slope_ref, gsub_ref, qt_ref, k_ref, vt_ref, o_ref,
                      qs_ref, bias_ref, m_ref, acc_ref, *, lambda_init):
    t = ATTN_TILE
    dv = 2 * DIFF_HEAD_DIM
    i = pl.program_id(2)
    first_map = lax.broadcasted_iota(jnp.int32, (HEAD_LANES, t), 0) < DIFF_HEAD_DIM
    key = lax.broadcasted_iota(jnp.int32, (t, t), 0)
    query = lax.broadcasted_iota(jnp.int32, (t, t), 1)
    in_tile_dist = (key - query).astype(F32)
    for e in range(DIFF_GROUP):
        q = qt_ref[0, e]
        zero = jnp.zeros_like(q)
        qs_ref[e, :, pl.ds(0, t)] = jnp.where(first_map, q, zero)
        qs_ref[e, :, pl.ds(t, t)] = jnp.where(first_map, zero, q)
        bias_ref[e] = in_tile_dist * slope_ref[e][:, 0:1]
    m_ref[...] = jnp.full(m_ref.shape, NEG_BIG, F32)
    acc_ref[...] = jnp.zeros(acc_ref.shape, F32)

    def step(j, masked):
        start = pl.multiple_of(j * t, t)
        tiles_apart = ((j - i) * t).astype(F32)

        def logits(e):
            k_blk = k_ref[0, pl.ds(start, t), e * HEAD_LANES:(e + 1) * HEAD_LANES]
            s = jnp.dot(k_blk, qs_ref[e], preferred_element_type=F32)
            bias = bias_ref[e]
            maps = [s[:, :t] + bias, s[:, t:] + bias]
            if masked:
                maps = [jnp.where(key <= query, sm, NEG_BIG) for sm in maps]
            return jnp.concatenate(maps, axis=1)

        def softmax(e, s):
            tile_bias = slope_ref[e][:, 0:1] * tiles_apart
            return _softmax_tile(s, m_ref, e, tile_bias)

        _pipelined_heads(
            DIFF_GROUP, logits, softmax,
            lambda e, p, alpha: _accumulate(vt_ref[0, e, j], p, alpha, acc_ref, e))

    def body(j, carry):
        step(j, False)
        return carry

    lax.fori_loop(0, i, body, 0)
    step(i, True)

    lv = lam_ref[...]
    lam = (jnp.exp(jnp.sum(lv[0:1] * lv[1:2], axis=-1, keepdims=True))
           - jnp.exp(jnp.sum(lv[2:3] * lv[3:4], axis=-1, keepdims=True)) + lambda_init)
    for e in range(DIFF_GROUP):
        o_t = acc_ref[e, pl.ds(0, dv), :] / acc_ref[e, pl.ds(dv, 1), :]
        o = (o_t[:, :t] - lam * o_t[:, t:]).T
        o_ref[0, :, e * dv:(e + 1) * dv] = (_rms(o, gsub_ref[...]) * (1.0 - lambda_init)).astype(BF16)


def _diff_attn(qt, k, vt, lam_vecs, slopes, gsub, lambda_init):
    b, lp, _ = k.shape
    t = ATTN_TILE
    g = DIFF_GROUP
    nblk = lp // t
    return pl.pallas_call(
        functools.partial(_diff_attn_kernel, lambda_init=lambda_init),
        grid=(b, DIFF_HEADS // g, nblk),
        in_specs=[_const_spec(lam_vecs.shape),
                  pl.BlockSpec((g, 1, HEAD_LANES), lambda bi, gi, i: (gi, 0, 0)),
                  _const_spec(gsub.shape),
                  pl.BlockSpec((1, g, HEAD_LANES, t), lambda bi, gi, i: (bi, gi, 0, i)),
                  pl.BlockSpec((1, lp, g * HEAD_LANES), lambda bi, gi, i: (bi, 0, gi),
                               pipeline_mode=pl.Buffered(1)),
                  pl.BlockSpec((1, g, nblk, DIFF_VT_ROWS, t), lambda bi, gi, i: (bi, gi, 0, 0, 0),
                               pipeline_mode=pl.Buffered(1))],
        out_specs=pl.BlockSpec((1, t, g * HEAD_LANES), lambda bi, gi, i: (bi, i, gi)),
        out_shape=jax.ShapeDtypeStruct((b, lp, DIFF_HEADS * HEAD_LANES), BF16),
        scratch_shapes=[pltpu.VMEM((g, HEAD_LANES, 2 * t), BF16), pltpu.VMEM((g, t, t), F32),
                        pltpu.VMEM((g, 1, 2 * t), F32), pltpu.VMEM((g, DIFF_VT_ROWS, 2 * t), F32)],
        compiler_params=_params(("parallel", "parallel", "arbitrary")),
        name="diff_attn",
    )(lam_vecs, slopes, gsub, qt, k, vt)


SUBLANES = 8
LANES = 128
FFN_HALO = 2 * SUBLANES


def _interleaved_rows(tm):
    return tm // SUBLANES


def _store_lane_tiles(ref, x):
    for l in range(ref.shape[0]):
        ref[l] = x[:, l * LANES:(l + 1) * LANES]


def _strided_rows(ref, start, n, stride):
    return jnp.concatenate([ref[l, pl.ds(start, n, stride=stride), :] for l in range(ref.shape[0])], axis=1)


def _post_kernel(h_ref, o_ref, wo_ref, g_ref, wup_ref, cw_ref, wdn_ref, out_ref,
                 nat_ref, hn_ref, xs_ref, carry_ref, f_ref, *, tm):
    fc = FFN_CHUNK
    run = _interleaved_rows(tm)

    @pl.when(pl.program_id(1) == 0)
    def _():
        carry_ref[...] = jnp.zeros(carry_ref.shape, F32)

    g = g_ref[...]
    m = jnp.dot(o_ref[0], wo_ref[...], preferred_element_type=F32)
    h1 = h_ref[0] + _rms(m, g[1:2])
    out_ref[0] = h1
    _store_lane_tiles(nat_ref, _rms(h1, g[2:3]))
    for r in range(0, run, 2):
        pair = [_strided_rows(nat_ref, r + k, SUBLANES, run) for k in range(2)]
        hn_ref[pl.ds(r * SUBLANES, 2 * SUBLANES), :] = jnp.concatenate(pair, axis=0).astype(BF16)
    f_ref[...] = jnp.zeros(f_ref.shape, F32)
    first_sublane = lax.broadcasted_iota(jnp.int32, (SUBLANES, 2 * fc), 0) == 0

    def up_proj(c):
        hn = hn_ref[...]
        gate = jnp.dot(hn, wup_ref[:, c * fc:(c + 1) * fc], preferred_element_type=F32)
        value = jnp.dot(hn, wup_ref[:, D_FF + c * fc:D_FF + (c + 1) * fc], preferred_element_type=F32)
        return gate, value

    def conv_act(c, up):
        xs = xs_ref.at[c % 2]
        xs[pl.ds(FFN_HALO, tm), pl.ds(0, fc)] = up[0]
        xs[pl.ds(FFN_HALO, tm), pl.ds(fc, fc)] = up[1]
        for k in range(2):
            cur = pltpu.roll(xs[pl.ds(tm + k * SUBLANES, SUBLANES), :], 1, 0)
            prev = pltpu.roll(carry_ref[c, pl.ds(k * SUBLANES, SUBLANES), :], 1, 0)
            xs[pl.ds(k * SUBLANES, SUBLANES), :] = jnp.where(first_sublane, prev, cur)
        carry_ref[c] = xs[pl.ds(tm, FFN_HALO), :]
        cw = jnp.concatenate([cw_ref[:, c * fc:(c + 1) * fc],
                              cw_ref[:, D_FF + c * fc:D_FF + (c + 1) * fc]], axis=1)
        y = (cw[2:3] * xs[pl.ds(FFN_HALO, tm), :] + cw[1:2] * xs[pl.ds(SUBLANES, tm), :]
             + cw[0:1] * xs[pl.ds(0, tm), :])
        half_gate = 0.5 * y[:, :fc]
        silu = half_gate + half_gate * jnp.tanh(half_gate)
        return (silu * y[:, fc:]).astype(BF16)

    nch = D_FF // fc
    up_next = up_proj(0)
    for c in range(nch):
        up = up_next
        if c + 1 < nch:
            up_next = up_proj(c + 1)
        act = conv_act(c, up)
        f_ref[...] += jnp.dot(act, wdn_ref[c * fc:(c + 1) * fc, :], preferred_element_type=F32)
    _store_lane_tiles(nat_ref, _rms(f_ref[...], g[3:4]))
    for j in range(tm // SUBLANES):
        pieces, t = [], j * SUBLANES
        while t < (j + 1) * SUBLANES:
            s_, r_ = divmod(t, run)
            n = min((j + 1) * SUBLANES - t, run - r_)
            pieces.append(_strided_rows(nat_ref, r_ * SUBLANES + s_, n, SUBLANES))
            t += n
        rows = pl.ds(j * SUBLANES, SUBLANES)
        out_ref[0, rows, :] = out_ref[0, rows, :] + (pieces[0] if len(pieces) == 1 else jnp.concatenate(pieces, axis=0))


def _post(h, o, wo, g, wup, cw, wdn, tm):
    b, lp, d = h.shape
    nch = D_FF // FFN_CHUNK
    tok = pl.BlockSpec((1, tm, d), lambda bi, i: (bi, i, 0))
    return pl.pallas_call(
        functools.partial(_post_kernel, tm=tm),
        grid=(b, lp // tm),
        in_specs=[tok, tok, _const_spec(wo.shape), _const_spec(g.shape), _const_spec(wup.shape),
                  _const_spec(cw.shape), _const_spec(wdn.shape)],
        out_specs=tok,
        out_shape=jax.ShapeDtypeStruct((b, lp, d), F32),
        scratch_shapes=[pltpu.VMEM((d // LANES, tm, LANES), F32), pltpu.VMEM((tm, d), BF16),
                        pltpu.VMEM((2, tm + FFN_HALO, 2 * FFN_CHUNK), F32),
                        pltpu.VMEM((nch, FFN_HALO, 2 * FFN_CHUNK), F32), pltpu.VMEM((tm, d), F32)],
        compiler_params=_params(("parallel", "arbitrary")),
        name="post_ffn",
    )(h, o, wo, g, wup, cw, wdn)


def _mla_weights(w_in, g_q, g_kv, w_uq, w_ukv):
    qr, kvr, r2 = MLA_Q_RANK, MLA_KV_RANK, MLA_ROPE // 2
    kr = w_in[:, qr + kvr:]
    pad = lambda a, lo: jnp.pad(a, ((0, 0), (lo, HEAD_LANES - lo - a.shape[1])))
    kr_swapped = jnp.concatenate([kr[:, r2:], kr[:, :r2]], axis=1)
    win = jnp.concatenate([w_in[:, :qr + kvr], pad(kr, MLA_NOPE), pad(kr_swapped, MLA_NOPE)], axis=1)
    wq = w_uq.reshape(qr, MLA_HEADS, MLA_QK)
    wqa = jnp.pad(wq, ((0, 0), (0, 0), (0, HEAD_LANES - MLA_QK)))
    rope_swapped = jnp.concatenate([wq[..., MLA_NOPE + r2:], wq[..., MLA_NOPE:MLA_NOPE + r2]], axis=-1)
    wqb = jnp.pad(rope_swapped, ((0, 0), (0, 0), (MLA_NOPE, HEAD_LANES - MLA_QK)))
    wkv = w_ukv.reshape(kvr, MLA_HEADS, MLA_NOPE + MLA_V)
    wk = jnp.pad(wkv[..., :MLA_NOPE], ((0, 0), (0, 0), (0, HEAD_LANES - MLA_NOPE)))
    wv = wkv[..., MLA_NOPE:]
    flat = lambda a: a.reshape(a.shape[0], -1).astype(BF16)
    return dict(win=win.astype(BF16), gq=g_q[None], gkv=g_kv[None], wqa=flat(wqa).T, wqb=flat(wqb).T,
                wk=flat(wk), wv=flat(wv).T)


def _rope_tables(lp):
    inv_freq = ROPE_THETA ** (-jnp.arange(0, MLA_ROPE, 2, dtype=F32) / MLA_ROPE)
    ang = jnp.arange(lp, dtype=F32)[:, None] * inv_freq[None, :]
    cos, sin = jnp.cos(ang), jnp.sin(ang)
    lay = lambda nope, a, b_: jnp.concatenate(
        [jnp.full((lp, MLA_NOPE), nope, F32), a, b_, jnp.zeros((lp, HEAD_LANES - MLA_QK), F32)], axis=1)
    scale = MLA_QK ** -0.5 * LOG2E
    return ((lay(1.0, cos, cos) * scale).T, (lay(0.0, -sin, sin) * scale).T,
            lay(0.0, cos, cos), lay(0.0, -sin, sin))


def kernel(x, meta_tokens, norms, mla_w_in, mla_norm_q, mla_norm_kv, mla_w_uq, mla_w_ukv, mla_w_o, sc_w_in, sc_conv, sc_w_out, diff_w_in, diff_lambda_q1, diff_lambda_k1, diff_lambda_q2, diff_lambda_k2, diff_subln, diff_w_o, ffn_w_up, ffn_conv, ffn_w_down):
    b, seq, d = x.shape
    depth = norms.shape[0]
    length = N_META + seq
    lp = -(-length // ATTN_TILE) * ATTN_TILE
    tm = _token_tile(lp)
    meta = jnp.broadcast_to(meta_tokens[None].astype(x.dtype), (b, N_META, d))
    h = jnp.concatenate([meta, x, jnp.zeros((b, lp - length, d), x.dtype)], axis=1)
    tabs = _rope_tables(lp)
    slopes = 2.0 ** (-8.0 * jnp.arange(1, DIFF_HEADS + 1, dtype=F32) / DIFF_HEADS) * LOG2E
    slopes = jnp.broadcast_to(slopes[:, None, None], (DIFF_HEADS, 1, HEAD_LANES))

    for i in range(depth):
        kind, j = i % N_MIXERS, i // N_MIXERS
        g = norms[i]
        if kind == 0:
            w = _mla_weights(mla_w_in[j], mla_norm_q[j], mla_norm_kv[j], mla_w_uq[j], mla_w_ukv[j])
            o = _mla_attn(*_mla_pre(h, g[0:1], w, tabs))
            wo = mla_w_o[j]
        elif kind == 1:
            o = _sc_pre(h, g[0:1], sc_w_in[j].astype(BF16), sc_conv[j], tm)
            wo = sc_w_out[j]
        else:
            lambda_init = 0.8 - 0.6 * math.exp(-0.3 * i)
            w_in = diff_w_in[j].astype(BF16)
            qt, k, vt = _diff_pre(h, g[0:1], w_in[:, :d].T, w_in[:, d:2 * d], w_in[:, 2 * d:].T)
            lam_vecs = jnp.stack([diff_lambda_q1[j], diff_lambda_k1[j], diff_lambda_q2[j], diff_lambda_k2[j]])
            o = _diff_attn(qt, k, vt, lam_vecs, slopes, diff_subln[j][None], lambda_init)
            wo = diff_w_o[j]
        h = _post(h, o, wo.astype(BF16), g, ffn_w_up[i].astype(BF16), ffn_conv[i],
                  ffn_w_down[i].astype(BF16), tm)
    return h[:, N_META:length]
```

```python
import functools
import math

import jax
import jax.numpy as jnp
from jax import lax
from jax.experimental import pallas as pl
from jax.experimental.pallas import tpu as pltpu

F32 = jnp.float32
BF16 = jnp.bfloat16

D_MODEL = 1024
N_META = 16
EPS = 1e-6
N_MIXERS = 3
LOG2E = math.log2(math.e)

MLA_HEADS = 16
MLA_Q_RANK = 256
MLA_KV_RANK = 128
MLA_NOPE = 64
MLA_ROPE = 32
MLA_V = 64
MLA_QK = MLA_NOPE + MLA_ROPE
ROPE_THETA = 10000.0
HEAD_LANES = 128
ONES_ROWS = 16
MLA_GROUP = 16
MLA_VT_ROWS = MLA_V + ONES_ROWS

DIFF_HEADS = 8
DIFF_HEAD_DIM = 64
DIFF_GROUP = 8
DIFF_VT_ROWS = 2 * DIFF_HEAD_DIM + ONES_ROWS

D_FF = 2816
FFN_CHUNK = 512
KEY_TILES_PER_TRIP = 2
HALO = 8

ATTN_TILE = 256
LOGITS_AHEAD = 3
ACCUMULATE_BEHIND = 2
TOKEN_TILE_TARGET = 544
NEG_BIG = -1e30
VMEM_LIMIT = 52 * 1024 * 1024


def _rms(x, g):
    return x * lax.rsqrt(jnp.mean(x * x, axis=-1, keepdims=True) + EPS) * g


def _dot_nt(a, b):
    return lax.dot_general(a, b, (((1,), (1,)), ((), ())), preferred_element_type=F32)


def _token_tile(lp):
    best = 16
    for t in range(16, min(lp, TOKEN_TILE_TARGET) + 1, 16):
        if lp % t == 0:
            best = t
    return best


def _const_spec(shape):
    nd = len(shape)
    return pl.BlockSpec(shape, lambda *_: (0,) * nd, pipeline_mode=pl.Buffered(1))


def _params(sem):
    return pltpu.CompilerParams(dimension_semantics=sem, vmem_limit_bytes=VMEM_LIMIT)


def _causal_conv(xs_ref, cur, cw, tm):
    return (cw[2:3] * cur + cw[1:2] * xs_ref[pl.ds(HALO - 1, tm), :]
            + cw[0:1] * xs_ref[pl.ds(HALO - 2, tm), :])


def _store_vt(vt_ref, vt, heads, width):
    t = vt.shape[1]
    for hd in range(heads):
        vt_ref[0, hd, 0, pl.ds(0, width), :] = vt[hd * width:(hd + 1) * width].astype(BF16)
        vt_ref[0, hd, 0, pl.ds(width, ONES_ROWS), :] = jnp.ones((ONES_ROWS, t), BF16)


def _softmax_tile(s, m_ref, e, tile_bias):
    m_old = m_ref[e]
    m_blk = jnp.max(s, axis=0, keepdims=True)
    if tile_bias is not None:
        m_blk = m_blk + tile_bias
    m_new = jnp.maximum(m_old, m_blk)
    m_ref[e] = m_new
    shift = m_new if tile_bias is None else m_new - tile_bias
    return jnp.exp2(s - shift).astype(BF16), jnp.exp2(m_old - m_new)


def _accumulate(vt_blk, p, alpha, acc_ref, e):
    pv = jnp.dot(vt_blk, p, preferred_element_type=F32)
    acc_ref[e] = alpha * acc_ref[e] + pv


def _pipelined_heads(n, logits, softmax, accumulate):
    ahead = [logits(e) for e in range(min(LOGITS_AHEAD, n))]
    pending = []
    for e in range(n):
        s = ahead.pop(0)
        if e + LOGITS_AHEAD < n:
            ahead.append(logits(e + LOGITS_AHEAD))
        p, alpha = softmax(e, s)
        pending.append((e, p, alpha))
        if len(pending) > ACCUMULATE_BEHIND:
            accumulate(*pending.pop(0))
    for item in pending:
        accumulate(*item)


def _sweep_key_tiles(i, step):
    n = KEY_TILES_PER_TRIP

    def body(trip, carry):
        step([(n * trip + k, False) for k in range(n)])
        return carry

    lax.fori_loop(0, i // n, body, 0)
    for left in range(n):
        @pl.when(i % n == left)
        def _():
            step([(i - left + k, False) for k in range(left)] + [(i, True)])


def _causal_tile_mask(t):
    key = lax.broadcasted_iota(jnp.int32, (t, t), 0)
    query = lax.broadcasted_iota(jnp.int32, (t, t), 1)
    return key <= query


def _mla_pre_kernel(h_ref, g0_ref, win_ref, gq_ref, gkv_ref, wqa_ref, wqb_ref, wk_ref, wv_ref,
                    cq_ref, sq_ref, ck_ref, sk_ref, qt_ref, k_ref, vt_ref):
    hn = _rms(h_ref[0], g0_ref[...]).astype(BF16)
    c = jnp.dot(hn, win_ref[...], preferred_element_type=F32)
    cq = _rms(c[:, :MLA_Q_RANK], gq_ref[...]).astype(BF16)
    ckv = _rms(c[:, MLA_Q_RANK:MLA_Q_RANK + MLA_KV_RANK], gkv_ref[...]).astype(BF16)
    o = MLA_Q_RANK + MLA_KV_RANK
    k_rope = (c[:, o:o + HEAD_LANES] * ck_ref[...]
              + c[:, o + HEAD_LANES:o + 2 * HEAD_LANES] * sk_ref[...])
    ka = jnp.dot(ckv, wk_ref[...], preferred_element_type=F32)
    for hd in range(MLA_HEADS):
        d = slice(hd * HEAD_LANES, (hd + 1) * HEAD_LANES)
        k_ref[0, :, d] = (ka[:, d] + k_rope).astype(BF16)
    cq_t = cq_ref[...]
    sq_t = sq_ref[...]
    per = 4
    for c0 in range(0, MLA_HEADS, per):
        rows = slice(c0 * HEAD_LANES, (c0 + per) * HEAD_LANES)
        qa = _dot_nt(wqa_ref[rows, :], cq)
        qb = _dot_nt(wqb_ref[rows, :], cq)
        for e in range(per):
            a = slice(e * HEAD_LANES, (e + 1) * HEAD_LANES)
            qt_ref[0, c0 + e] = (qa[a] * cq_t + qb[a] * sq_t).astype(BF16)
    _store_vt(vt_ref, _dot_nt(wv_ref[...], ckv), MLA_HEADS, MLA_V)


def _mla_pre(h, g0, w, tabs):
    b, lp, d = h.shape
    t = ATTN_TILE
    nblk = lp // t
    tok = lambda n: pl.BlockSpec((1, t, n), lambda bi, i: (bi, i, 0))
    tab = pl.BlockSpec((t, HEAD_LANES), lambda bi, i: (i, 0))
    tab_t = pl.BlockSpec((HEAD_LANES, t), lambda bi, i: (0, i))
    return pl.pallas_call(
        _mla_pre_kernel,
        grid=(b, nblk),
        in_specs=[tok(d), _const_spec(g0.shape), _const_spec(w["win"].shape),
                  _const_spec(w["gq"].shape), _const_spec(w["gkv"].shape),
                  _const_spec(w["wqa"].shape), _const_spec(w["wqb"].shape),
                  _const_spec(w["wk"].shape), _const_spec(w["wv"].shape), tab_t, tab_t, tab, tab],
        out_specs=[pl.BlockSpec((1, MLA_HEADS, HEAD_LANES, t), lambda bi, i: (bi, 0, 0, i)),
                   tok(MLA_HEADS * HEAD_LANES),
                   pl.BlockSpec((1, MLA_HEADS, 1, MLA_VT_ROWS, t), lambda bi, i: (bi, 0, i, 0, 0))],
        out_shape=[jax.ShapeDtypeStruct((b, MLA_HEADS, HEAD_LANES, lp), BF16),
                   jax.ShapeDtypeStruct((b, lp, MLA_HEADS * HEAD_LANES), BF16),
                   jax.ShapeDtypeStruct((b, MLA_HEADS, nblk, MLA_VT_ROWS, t), BF16)],
        compiler_params=_params(("parallel", "parallel")),
        name="mla_pre",
    )(h, g0, w["win"], w["gq"], w["gkv"], w["wqa"], w["wqb"], w["wk"], w["wv"], *tabs)


def _mla_attn_kernel(qt_ref, k_ref, vt_ref, o_ref, m_ref, acc_ref):
    t = ATTN_TILE
    i = pl.program_id(2)
    m_ref[...] = jnp.full(m_ref.shape, NEG_BIG, F32)
    acc_ref[...] = jnp.zeros(acc_ref.shape, F32)

    def step(tiles):
        g = MLA_GROUP

        def logits(v):
            (j, masked), e = tiles[v // g], v % g
            k_blk = k_ref[0, pl.ds(pl.multiple_of(j * t, t), t), e * HEAD_LANES:(e + 1) * HEAD_LANES]
            s = jnp.dot(k_blk, qt_ref[0, e], preferred_element_type=F32)
            return jnp.where(_causal_tile_mask(t), s, NEG_BIG) if masked else s

        _pipelined_heads(
            g * len(tiles), logits,
            lambda v, s: _softmax_tile(s, m_ref, v % g, None),
            lambda v, p, alpha: _accumulate(vt_ref[0, v % g, tiles[v // g][0]], p, alpha, acc_ref, v % g))

    _sweep_key_tiles(i, step)
    for p in range(MLA_GROUP // 2):
        halves = [acc_ref[2 * p + e, pl.ds(0, MLA_V), :] / acc_ref[2 * p + e, pl.ds(MLA_V, 1), :]
                  for e in range(2)]
        o_ref[0, :, p * HEAD_LANES:(p + 1) * HEAD_LANES] = jnp.concatenate(halves, axis=0).T.astype(BF16)


def _mla_attn(qt, k, vt):
    b, lp, _ = k.shape
    t = ATTN_TILE
    g = MLA_GROUP
    nblk = lp // t
    return pl.pallas_call(
        _mla_attn_kernel,
        grid=(b, MLA_HEADS // g, nblk),
        in_specs=[pl.BlockSpec((1, g, HEAD_LANES, t), lambda bi, gi, i: (bi, gi, 0, i)),
                  pl.BlockSpec((1, lp, g * HEAD_LANES), lambda bi, gi, i: (bi, 0, gi),
                               pipeline_mode=pl.Buffered(1)),
                  pl.BlockSpec((1, g, nblk, MLA_VT_ROWS, t), lambda bi, gi, i: (bi, gi, 0, 0, 0),
                               pipeline_mode=pl.Buffered(1))],
        out_specs=pl.BlockSpec((1, t, g * MLA_V), lambda bi, gi, i: (bi, i, gi)),
        out_shape=jax.ShapeDtypeStruct((b, lp, MLA_HEADS * MLA_V), BF16),
        scratch_shapes=[pltpu.VMEM((g, 1, t), F32), pltpu.VMEM((g, MLA_VT_ROWS, t), F32)],
        compiler_params=_params(("parallel", "parallel", "arbitrary")),
        name="mla_attn",
    )(qt, k, vt)


def _sc_pre_kernel(h_ref, g_ref, w_ref, cw_ref, o_ref, xs_ref, *, tm):
    d = D_MODEL

    @pl.when(pl.program_id(1) == 0)
    def _():
        xs_ref[pl.ds(tm, HALO), :] = jnp.zeros((HALO, d), F32)

    hn = _rms(h_ref[0], g_ref[...]).astype(BF16)
    gate_c = jnp.dot(hn, w_ref[:, d:2 * d], preferred_element_type=F32)
    u = jnp.dot(hn, w_ref[:, 2 * d:], preferred_element_type=F32)
    z = gate_c * u
    xs_ref[pl.ds(0, HALO), :] = xs_ref[pl.ds(tm, HALO), :]
    xs_ref[pl.ds(HALO, tm), :] = z
    y = _causal_conv(xs_ref, z, cw_ref[...], tm)
    gate_b = jnp.dot(hn, w_ref[:, :d], preferred_element_type=F32)
    o_ref[0] = (gate_b * y).astype(BF16)


def _sc_pre(h, g, w, cw, tm):
    b, lp, d = h.shape
    tok = pl.BlockSpec((1, tm, d), lambda bi, i: (bi, i, 0))
    return pl.pallas_call(
        functools.partial(_sc_pre_kernel, tm=tm),
        grid=(b, lp // tm),
        in_specs=[tok, _const_spec(g.shape), _const_spec(w.shape), _const_spec(cw.shape)],
        out_specs=tok,
        out_shape=jax.ShapeDtypeStruct((b, lp, d), BF16),
        scratch_shapes=[pltpu.VMEM((tm + HALO, d), F32)],
        compiler_params=_params(("parallel", "arbitrary")),
        name="sc_pre",
    )(h, g, w, cw)


def _diff_pre_kernel(h_ref, g_ref, wqt_ref, wk_ref, wvt_ref, qt_ref, k_ref, vt_ref):
    hn = _rms(h_ref[0], g_ref[...]).astype(BF16)
    k_ref[0] = jnp.dot(hn, wk_ref[...], preferred_element_type=F32).astype(BF16)
    qt = _dot_nt(wqt_ref[...], hn) * (DIFF_HEAD_DIM ** -0.5 * LOG2E)
    for hd in range(DIFF_HEADS):
        qt_ref[0, hd] = qt[hd * HEAD_LANES:(hd + 1) * HEAD_LANES].astype(BF16)
    _store_vt(vt_ref, _dot_nt(wvt_ref[...], hn), DIFF_HEADS, 2 * DIFF_HEAD_DIM)


def _diff_pre(h, g, wqt, wk, wvt):
    b, lp, d = h.shape
    t = ATTN_TILE
    nblk = lp // t
    tok = pl.BlockSpec((1, t, d), lambda bi, i: (bi, i, 0))
    return pl.pallas_call(
        _diff_pre_kernel,
        grid=(b, nblk),
        in_specs=[tok, _const_spec(g.shape), _const_spec(wqt.shape), _const_spec(wk.shape),
                  _const_spec(wvt.shape)],
        out_specs=[pl.BlockSpec((1, DIFF_HEADS, HEAD_LANES, t), lambda bi, i: (bi, 0, 0, i)), tok,
                   pl.BlockSpec((1, DIFF_HEADS, 1, DIFF_VT_ROWS, t), lambda bi, i: (bi, 0, i, 0, 0))],
        out_shape=[jax.ShapeDtypeStruct((b, DIFF_HEADS, HEAD_LANES, lp), BF16),
                   jax.ShapeDtypeStruct((b, lp, d), BF16),
                   jax.ShapeDtypeStruct((b, DIFF_HEADS, nblk, DIFF_VT_ROWS, t), BF16)],
        compiler_params=_params(("parallel", "parallel")),
        name="diff_pre",
    )(h, g, wqt, wk, wvt)


def _diff_attn_kernel(lam_ref, slope_ref, gsub_ref, qt_ref, k_ref, vt_ref, o_ref,
                      qs_ref, bias_ref, m_ref, acc_ref, *, lambda_init):
    t = ATTN_TILE
    dv = 2 * DIFF_HEAD_DIM
    i = pl.program_id(2)
    first_map = lax.broadcasted_iota(jnp.int32, (HEAD_LANES, t), 0) < DIFF_HEAD_DIM
    key = lax.broadcasted_iota(jnp.int32, (t, t), 0)
    query = lax.broadcasted_iota(jnp.int32, (t, t), 1)
    in_tile_dist = (key - query).astype(F32)
    for e in range(DIFF_GROUP):
        q = qt_ref[0, e]
        zero = jnp.zeros_like(q)
        qs_ref[e, :, pl.ds(0, t)] = jnp.where(first_map, q, zero)
        qs_ref[e, :, pl.ds(t, t)] = jnp.where(first_map, zero, q)
        bias_ref[e] = in_tile_dist * slope_ref[e][:, 0:1]
    m_ref[...] = jnp.full(m_ref.shape, NEG_BIG, F32)
    acc_ref[...] = jnp.zeros(acc_ref.shape, F32)

    def step(tiles):
        g = DIFF_GROUP

        def logits(v):
            (j, masked), e = tiles[v // g], v % g
            k_blk = k_ref[0, pl.ds(pl.multiple_of(j * t, t), t), e * HEAD_LANES:(e + 1) * HEAD_LANES]
            s = jnp.dot(k_blk, qs_ref[e], preferred_element_type=F32)
            bias = bias_ref[e]
            maps = [s[:, :t] + bias, s[:, t:] + bias]
            if masked:
                maps = [jnp.where(key <= query, sm, NEG_BIG) for sm in maps]
            return jnp.concatenate(maps, axis=1)

        def softmax(v, s):
            j, e = tiles[v // g][0], v % g
            tile_bias = slope_ref[e][:, 0:1] * ((j - i) * t).astype(F32)
            return _softmax_tile(s, m_ref, e, tile_bias)

        _pipelined_heads(
            g * len(tiles), logits, softmax,
            lambda v, p, alpha: _accumulate(vt_ref[0, v % g, tiles[v // g][0]], p, alpha, acc_ref, v % g))

    _sweep_key_tiles(i, step)

    lv = lam_ref[...]
    lam = (jnp.exp(jnp.sum(lv[0:1] * lv[1:2], axis=-1, keepdims=True))
           - jnp.exp(jnp.sum(lv[2:3] * lv[3:4], axis=-1, keepdims=True)) + lambda_init)
    for e in range(DIFF_GROUP):
        o_t = acc_ref[e, pl.ds(0, dv), :] / acc_ref[e, pl.ds(dv, 1), :]
        o = (o_t[:, :t] - lam * o_t[:, t:]).T
        o_ref[0, :, e * dv:(e + 1) * dv] = (_rms(o, gsub_ref[...]) * (1.0 - lambda_init)).astype(BF16)


def _diff_attn(qt, k, vt, lam_vecs, slopes, gsub, lambda_init):
    b, lp, _ = k.shape
    t = ATTN_TILE
    g = DIFF_GROUP
    nblk = lp // t
    return pl.pallas_call(
        functools.partial(_diff_attn_kernel, lambda_init=lambda_init),
        grid=(b, DIFF_HEADS // g, nblk),
        in_specs=[_const_spec(lam_vecs.shape),
                  pl.BlockSpec((g, 1, HEAD_LANES), lambda bi, gi, i: (gi, 0, 0)),
                  _const_spec(gsub.shape),
                  pl.BlockSpec((1, g, HEAD_LANES, t), lambda bi, gi, i: (bi, gi, 0, i)),
                  pl.BlockSpec((1, lp, g * HEAD_LANES), lambda bi, gi, i: (bi, 0, gi),
                               pipeline_mode=pl.Buffered(1)),
                  pl.BlockSpec((1, g, nblk, DIFF_VT_ROWS, t), lambda bi, gi, i: (bi, gi, 0, 0, 0),
                               pipeline_mode=pl.Buffered(1))],
        out_specs=pl.BlockSpec((1, t, g * HEAD_LANES), lambda bi, gi, i: (bi, i, gi)),
        out_shape=jax.ShapeDtypeStruct((b, lp, DIFF_HEADS * HEAD_LANES), BF16),
        scratch_shapes=[pltpu.VMEM((g, HEAD_LANES, 2 * t), BF16), pltpu.VMEM((g, t, t), F32),
                        pltpu.VMEM((g, 1, 2 * t), F32), pltpu.VMEM((g, DIFF_VT_ROWS, 2 * t), F32)],
        compiler_params=_params(("parallel", "parallel", "arbitrary")),
        name="diff_attn",
    )(lam_vecs, slopes, gsub, qt, k, vt)


SUBLANES = 8
LANES = 128
FFN_HALO = 2 * SUBLANES


def _interleaved_rows(tm):
    return tm // SUBLANES


def _ffn_chunks():
    return [(lo, min(FFN_CHUNK, D_FF - lo)) for lo in range(0, D_FF, FFN_CHUNK)]


def _store_lane_tiles(ref, x):
    for l in range(ref.shape[0]):
        ref[l] = x[:, l * LANES:(l + 1) * LANES]


def _strided_rows(ref, start, n, stride):
    return jnp.concatenate([ref[l, pl.ds(start, n, stride=stride), :] for l in range(ref.shape[0])], axis=1)


def _post_kernel(h_ref, o_ref, wo_ref, g_ref, wup_ref, cw_ref, wdn_ref, out_ref,
                 nat_ref, hn_ref, xs_ref, carry_ref, f_ref, *, tm):
    chunks = _ffn_chunks()
    run = _interleaved_rows(tm)

    @pl.when(pl.program_id(1) == 0)
    def _():
        carry_ref[...] = jnp.zeros(carry_ref.shape, F32)

    g = g_ref[...]
    m = jnp.dot(o_ref[0], wo_ref[...], preferred_element_type=F32)
    h1 = h_ref[0] + _rms(m, g[1:2])
    out_ref[0] = h1
    _store_lane_tiles(nat_ref, _rms(h1, g[2:3]))
    for r in range(0, run, 2):
        pair = [_strided_rows(nat_ref, r + k, SUBLANES, run) for k in range(2)]
        hn_ref[pl.ds(r * SUBLANES, 2 * SUBLANES), :] = jnp.concatenate(pair, axis=0).astype(BF16)
    f_ref[...] = jnp.zeros(f_ref.shape, F32)

    def up_proj(c):
        lo, w = chunks[c]
        hn = hn_ref[...]
        gate = jnp.dot(hn, wup_ref[:, lo:lo + w], preferred_element_type=F32)
        value = jnp.dot(hn, wup_ref[:, D_FF + lo:D_FF + lo + w], preferred_element_type=F32)
        return gate, value

    def conv_act(c, up):
        lo, w = chunks[c]
        xs = xs_ref.at[c % 2]
        cols = pl.ds(0, 2 * w)
        xs[pl.ds(FFN_HALO, tm), pl.ds(0, w)] = up[0]
        xs[pl.ds(FFN_HALO, tm), pl.ds(w, w)] = up[1]
        first_sublane = lax.broadcasted_iota(jnp.int32, (SUBLANES, 2 * w), 0) == 0
        for k in range(2):
            cur = pltpu.roll(xs[pl.ds(tm + k * SUBLANES, SUBLANES), cols], 1, 0)
            prev = pltpu.roll(carry_ref[c, pl.ds(k * SUBLANES, SUBLANES), cols], 1, 0)
            xs[pl.ds(k * SUBLANES, SUBLANES), cols] = jnp.where(first_sublane, prev, cur)
        carry_ref[c, :, cols] = xs[pl.ds(tm, FFN_HALO), cols]
        cw = jnp.concatenate([cw_ref[:, lo:lo + w], cw_ref[:, D_FF + lo:D_FF + lo + w]], axis=1)
        y = (cw[2:3] * xs[pl.ds(FFN_HALO, tm), cols] + cw[1:2] * xs[pl.ds(SUBLANES, tm), cols]
             + cw[0:1] * xs[pl.ds(0, tm), cols])
        half_gate = 0.5 * y[:, :w]
        silu = half_gate + half_gate * jnp.tanh(half_gate)
        return (silu * y[:, w:]).astype(BF16)

    nch = len(chunks)
    up_next = up_proj(0)
    for c in range(nch):
        up = up_next
        if c + 1 < nch:
            up_next = up_proj(c + 1)
        act = conv_act(c, up)
        lo, w = chunks[c]
        f_ref[...] += jnp.dot(act, wdn_ref[lo:lo + w, :], preferred_element_type=F32)
    _store_lane_tiles(nat_ref, _rms(f_ref[...], g[3:4]))
    for j in range(tm // SUBLANES):
        pieces, t = [], j * SUBLANES
        while t < (j + 1) * SUBLANES:
            s_, r_ = divmod(t, run)
            n = min((j + 1) * SUBLANES - t, run - r_)
            pieces.append(_strided_rows(nat_ref, r_ * SUBLANES + s_, n, SUBLANES))
            t += n
        rows = pl.ds(j * SUBLANES, SUBLANES)
        out_ref[0, rows, :] = out_ref[0, rows, :] + (pieces[0] if len(pieces) == 1 else jnp.concatenate(pieces, axis=0))


def _post(h, o, wo, g, wup, cw, wdn, tm):
    b, lp, d = h.shape
    nch = len(_ffn_chunks())
    tok = pl.BlockSpec((1, tm, d), lambda bi, i: (bi, i, 0))
    return pl.pallas_call(
        functools.partial(_post_kernel, tm=tm),
        grid=(b, lp // tm),
        in_specs=[tok, tok, _const_spec(wo.shape), _const_spec(g.shape), _const_spec(wup.shape),
                  _const_spec(cw.shape), _const_spec(wdn.shape)],
        out_specs=tok,
        out_shape=jax.ShapeDtypeStruct((b, lp, d), F32),
        scratch_shapes=[pltpu.VMEM((d // LANES, tm, LANES), F32), pltpu.VMEM((tm, d), BF16),
                        pltpu.VMEM((2, tm + FFN_HALO, 2 * FFN_CHUNK), F32),
                        pltpu.VMEM((nch, FFN_HALO, 2 * FFN_CHUNK), F32), pltpu.VMEM((tm, d), F32)],
        compiler_params=_params(("parallel", "arbitrary")),
        name="post_ffn",
    )(h, o, wo, g, wup, cw, wdn)


def _mla_weights(w_in, g_q, g_kv, w_uq, w_ukv):
    qr, kvr, r2 = MLA_Q_RANK, MLA_KV_RANK, MLA_ROPE // 2
    kr = w_in[:, qr + kvr:]
    pad = lambda a, lo: jnp.pad(a, ((0, 0), (lo, HEAD_LANES - lo - a.shape[1])))
    kr_swapped = jnp.concatenate([kr[:, r2:], kr[:, :r2]], axis=1)
    win = jnp.concatenate([w_in[:, :qr + kvr], pad(kr, MLA_NOPE), pad(kr_swapped, MLA_NOPE)], axis=1)
    wq = w_uq.reshape(qr, MLA_HEADS, MLA_QK)
    wqa = jnp.pad(wq, ((0, 0), (0, 0), (0, HEAD_LANES - MLA_QK)))
    rope_swapped = jnp.concatenate([wq[..., MLA_NOPE + r2:], wq[..., MLA_NOPE:MLA_NOPE + r2]], axis=-1)
    wqb = jnp.pad(rope_swapped, ((0, 0), (0, 0), (MLA_NOPE, HEAD_LANES - MLA_QK)))
    wkv = w_ukv.reshape(kvr, MLA_HEADS, MLA_NOPE + MLA_V)
    wk = jnp.pad(wkv[..., :MLA_NOPE], ((0, 0), (0, 0), (0, HEAD_LANES - MLA_NOPE)))
    wv = wkv[..., MLA_NOPE:]
    flat = lambda a: a.reshape(a.shape[0], -1).astype(BF16)
    return dict(win=win.astype(BF16), gq=g_q[None], gkv=g_kv[None], wqa=flat(wqa).T, wqb=flat(wqb).T,
                wk=flat(wk), wv=flat(wv).T)


def _rope_tables(lp):
    inv_freq = ROPE_THETA ** (-jnp.arange(0, MLA_ROPE, 2, dtype=F32) / MLA_ROPE)
    ang = jnp.arange(lp, dtype=F32)[:, None] * inv_freq[None, :]
    cos, sin = jnp.cos(ang), jnp.sin(ang)
    lay = lambda nope, a, b_: jnp.concatenate(
        [jnp.full((lp, MLA_NOPE), nope, F32), a, b_, jnp.zeros((lp, HEAD_LANES - MLA_QK), F32)], axis=1)
    scale = MLA_QK ** -0.5 * LOG2E
    return ((lay(1.0, cos, cos) * scale).T, (lay(0.0, -sin, sin) * scale).T,
            lay(0.0, cos, cos), lay(0.0, -sin, sin))


def kernel(x, meta_tokens, norms, mla_w_in, mla_norm_q, mla_norm_kv, mla_w_uq, mla_w_ukv, mla_w_o, sc_w_in, sc_conv, sc_w_out, diff_w_in, diff_lambda_q1, diff_lambda_k1, diff_lambda_q2, diff_lambda_k2, diff_subln, diff_w_o, ffn_w_up, ffn_conv, ffn_w_down):
    b, seq, d = x.shape
    depth = norms.shape[0]
    length = N_META + seq
    lp = -(-length // ATTN_TILE) * ATTN_TILE
    tm = _token_tile(lp)
    meta = jnp.broadcast_to(meta_tokens[None].astype(x.dtype), (b, N_META, d))
    h = jnp.concatenate([meta, x, jnp.zeros((b, lp - length, d), x.dtype)], axis=1)
    tabs = _rope_tables(lp)
    slopes = 2.0 ** (-8.0 * jnp.arange(1, DIFF_HEADS + 1, dtype=F32) / DIFF_HEADS) * LOG2E
    slopes = jnp.broadcast_to(slopes[:, None, None], (DIFF_HEADS, 1, HEAD_LANES))

    for i in range(depth):
        kind, j = i % N_MIXERS, i // N_MIXERS
        g = norms[i]
        if kind == 0:
            w = _mla_weights(mla_w_in[j], mla_norm_q[j], mla_norm_kv[j], mla_w_uq[j], mla_w_ukv[j])
            o = _mla_attn(*_mla_pre(h, g[0:1], w, tabs))
            wo = mla_w_o[j]
        elif kind == 1:
            o = _sc_pre(h, g[0:1], sc_w_in[j].astype(BF16), sc_conv[j], tm)
            wo = sc_w_out[j]
        else:
            lambda_init = 0.8 - 0.6 * math.exp(-0.3 * i)
            w_in = diff_w_in[j].astype(BF16)
            qt, k, vt = _diff_pre(h, g[0:1], w_in[:, :d].T, w_in[:, d:2 * d], w_in[:, 2 * d:].T)
            lam_vecs = jnp.stack([diff_lambda_q1[j], diff_lambda_k1[j], diff_lambda_q2[j], diff_lambda_k2[j]])
            o = _diff_attn(qt, k, vt, lam_vecs, slopes, diff_subln[j][None], lambda_init)
            wo = diff_w_o[j]
        h = _post(h, o, wo.astype(BF16), g, ffn_w_up[i].astype(BF16), ffn_conv[i],
                  ffn_w_down[i].astype(BF16), tm)
    return h[:, N_META:length]
```

```python
import functools
import math

import jax
import jax.numpy as jnp
from jax import lax
from jax.experimental import pallas as pl
from jax.experimental.pallas import tpu as pltpu

F32 = jnp.float32
BF16 = jnp.bfloat16

D_MODEL = 1024
N_META = 16
EPS = 1e-6
N_MIXERS = 3
LOG2E = math.log2(math.e)

MLA_HEADS = 16
MLA_Q_RANK = 256
MLA_KV_RANK = 128
MLA_NOPE = 64
MLA_ROPE = 32
MLA_V = 64
MLA_QK = MLA_NOPE + MLA_ROPE
ROPE_THETA = 10000.0
HEAD_LANES = 128
ONES_ROWS = 16
MLA_GROUP = 16
MLA_VT_ROWS = MLA_V + ONES_ROWS

DIFF_HEADS = 8
DIFF_HEAD_DIM = 64
DIFF_GROUP = 8
DIFF_VT_ROWS = 2 * DIFF_HEAD_DIM + ONES_ROWS

D_FF = 2816
FFN_CHUNK = 768
KEY_TILES_PER_TRIP = 4
HALO = 8

ATTN_TILE = 256
LOGITS_AHEAD = 3
ACCUMULATE_BEHIND = 2
TOKEN_TILE_TARGET = 544
NEG_BIG = -1e30
VMEM_LIMIT = 52 * 1024 * 1024


def _rms(x, g):
    return x * lax.rsqrt(jnp.mean(x * x, axis=-1, keepdims=True) + EPS) * g


def _dot_nt(a, b):
    return lax.dot_general(a, b, (((1,), (1,)), ((), ())), preferred_element_type=F32)


def _token_tile(lp):
    best = 16
    for t in range(16, min(lp, TOKEN_TILE_TARGET) + 1, 16):
        if lp % t == 0:
            best = t
    return best


def _const_spec(shape):
    nd = len(shape)
    return pl.BlockSpec(shape, lambda *_: (0,) * nd, pipeline_mode=pl.Buffered(1))


def _params(sem):
    return pltpu.CompilerParams(dimension_semantics=sem, vmem_limit_bytes=VMEM_LIMIT)


def _causal_conv(xs_ref, cur, cw, tm):
    return (cw[2:3] * cur + cw[1:2] * xs_ref[pl.ds(HALO - 1, tm), :]
            + cw[0:1] * xs_ref[pl.ds(HALO - 2, tm), :])


def _store_vt(vt_ref, vt, heads, width):
    t = vt.shape[1]
    for hd in range(heads):
        vt_ref[0, hd, 0, pl.ds(0, width), :] = vt[hd * width:(hd + 1) * width].astype(BF16)
        vt_ref[0, hd, 0, pl.ds(width, ONES_ROWS), :] = jnp.ones((ONES_ROWS, t), BF16)


def _softmax_tile(s, m_ref, e, tile_bias):
    m_old = m_ref[e]
    m_blk = jnp.max(s, axis=0, keepdims=True)
    if tile_bias is not None:
        m_blk = m_blk + tile_bias
    m_new = jnp.maximum(m_old, m_blk)
    m_ref[e] = m_new
    shift = m_new if tile_bias is None else m_new - tile_bias
    return jnp.exp2(s - shift).astype(BF16), jnp.exp2(m_old - m_new)


def _accumulate(vt_blk, p, alpha, acc_ref, e):
    pv = jnp.dot(vt_blk, p, preferred_element_type=F32)
    acc_ref[e] = alpha * acc_ref[e] + pv


def _pipelined_heads(n, logits, softmax, accumulate):
    ahead = [logits(e) for e in range(min(LOGITS_AHEAD, n))]
    pending = []
    for e in range(n):
        s = ahead.pop(0)
        if e + LOGITS_AHEAD < n:
            ahead.append(logits(e + LOGITS_AHEAD))
        p, alpha = softmax(e, s)
        pending.append((e, p, alpha))
        if len(pending) > ACCUMULATE_BEHIND:
            accumulate(*pending.pop(0))
    for item in pending:
        accumulate(*item)


def _sweep_key_tiles(i, step):
    n = KEY_TILES_PER_TRIP

    def body(trip, carry):
        step([(n * trip + k, False) for k in range(n)])
        return carry

    lax.fori_loop(0, i // n, body, 0)
    for left in range(n):
        @pl.when(i % n == left)
        def _():
            step([(i - left + k, False) for k in range(left)] + [(i, True)])


def _causal_tile_mask(t):
    key = lax.broadcasted_iota(jnp.int32, (t, t), 0)
    query = lax.broadcasted_iota(jnp.int32, (t, t), 1)
    return key <= query


def _mla_pre_kernel(h_ref, g0_ref, win_ref, gq_ref, gkv_ref, wqa_ref, wqb_ref, wk_ref, wv_ref,
                    cq_ref, sq_ref, ck_ref, sk_ref, qt_ref, k_ref, vt_ref):
    hn = _rms(h_ref[0], g0_ref[...]).astype(BF16)
    c = jnp.dot(hn, win_ref[...], preferred_element_type=F32)
    cq = _rms(c[:, :MLA_Q_RANK], gq_ref[...]).astype(BF16)
    ckv = _rms(c[:, MLA_Q_RANK:MLA_Q_RANK + MLA_KV_RANK], gkv_ref[...]).astype(BF16)
    o = MLA_Q_RANK + MLA_KV_RANK
    k_rope = (c[:, o:o + HEAD_LANES] * ck_ref[...]
              + c[:, o + HEAD_LANES:o + 2 * HEAD_LANES] * sk_ref[...])
    ka = jnp.dot(ckv, wk_ref[...], preferred_element_type=F32)
    for hd in range(MLA_HEADS):
        d = slice(hd * HEAD_LANES, (hd + 1) * HEAD_LANES)
        k_ref[0, :, d] = (ka[:, d] + k_rope).astype(BF16)
    cq_t = cq_ref[...]
    sq_t = sq_ref[...]
    per = 4
    for c0 in range(0, MLA_HEADS, per):
        rows = slice(c0 * HEAD_LANES, (c0 + per) * HEAD_LANES)
        qa = _dot_nt(wqa_ref[rows, :], cq)
        qb = _dot_nt(wqb_ref[rows, :], cq)
        for e in range(per):
            a = slice(e * HEAD_LANES, (e + 1) * HEAD_LANES)
            qt_ref[0, c0 + e] = (qa[a] * cq_t + qb[a] * sq_t).astype(BF16)
    _store_vt(vt_ref, _dot_nt(wv_ref[...], ckv), MLA_HEADS, MLA_V)


def _mla_pre(h, g0, w, tabs):
    b, lp, d = h.shape
    t = ATTN_TILE
    nblk = lp // t
    tok = lambda n: pl.BlockSpec((1, t, n), lambda bi, i: (bi, i, 0))
    tab = pl.BlockSpec((t, HEAD_LANES), lambda bi, i: (i, 0))
    tab_t = pl.BlockSpec((HEAD_LANES, t), lambda bi, i: (0, i))
    return pl.pallas_call(
        _mla_pre_kernel,
        grid=(b, nblk),
        in_specs=[tok(d), _const_spec(g0.shape), _const_spec(w["win"].shape),
                  _const_spec(w["gq"].shape), _const_spec(w["gkv"].shape),
                  _const_spec(w["wqa"].shape), _const_spec(w["wqb"].shape),
                  _const_spec(w["wk"].shape), _const_spec(w["wv"].shape), tab_t, tab_t, tab, tab],
        out_specs=[pl.BlockSpec((1, MLA_HEADS, HEAD_LANES, t), lambda bi, i: (bi, 0, 0, i)),
                   tok(MLA_HEADS * HEAD_LANES),
                   pl.BlockSpec((1, MLA_HEADS, 1, MLA_VT_ROWS, t), lambda bi, i: (bi, 0, i, 0, 0))],
        out_shape=[jax.ShapeDtypeStruct((b, MLA_HEADS, HEAD_LANES, lp), BF16),
                   jax.ShapeDtypeStruct((b, lp, MLA_HEADS * HEAD_LANES), BF16),
                   jax.ShapeDtypeStruct((b, MLA_HEADS, nblk, MLA_VT_ROWS, t), BF16)],
        compiler_params=_params(("parallel", "parallel")),
        name="mla_pre",
    )(h, g0, w["win"], w["gq"], w["gkv"], w["wqa"], w["wqb"], w["wk"], w["wv"], *tabs)


def _mla_attn_kernel(qt_ref, k_ref, vt_ref, o_ref, m_ref, acc_ref):
    t = ATTN_TILE
    i = pl.program_id(2)
    m_ref[...] = jnp.full(m_ref.shape, NEG_BIG, F32)
    acc_ref[...] = jnp.zeros(acc_ref.shape, F32)

    def step(tiles):
        g = MLA_GROUP

        def logits(v):
            (j, masked), e = tiles[v // g], v % g
            k_blk = k_ref[0, pl.ds(pl.multiple_of(j * t, t), t), e * HEAD_LANES:(e + 1) * HEAD_LANES]
            s = jnp.dot(k_blk, qt_ref[0, e], preferred_element_type=F32)
            return jnp.where(_causal_tile_mask(t), s, NEG_BIG) if masked else s

        _pipelined_heads(
            g * len(tiles), logits,
            lambda v, s: _softmax_tile(s, m_ref, v % g, None),
            lambda v, p, alpha: _accumulate(vt_ref[0, v % g, tiles[v // g][0]], p, alpha, acc_ref, v % g))

    _sweep_key_tiles(i, step)
    for p in range(MLA_GROUP // 2):
        halves = [acc_ref[2 * p + e, pl.ds(0, MLA_V), :] / acc_ref[2 * p + e, pl.ds(MLA_V, 1), :]
                  for e in range(2)]
        o_ref[0, :, p * HEAD_LANES:(p + 1) * HEAD_LANES] = jnp.concatenate(halves, axis=0).T.astype(BF16)


def _mla_attn(qt, k, vt):
    b, lp, _ = k.shape
    t = ATTN_TILE
    g = MLA_GROUP
    nblk = lp // t
    return pl.pallas_call(
        _mla_attn_kernel,
        grid=(b, MLA_HEADS // g, nblk),
        in_specs=[pl.BlockSpec((1, g, HEAD_LANES, t), lambda bi, gi, i: (bi, gi, 0, i)),
                  pl.BlockSpec((1, lp, g * HEAD_LANES), lambda bi, gi, i: (bi, 0, gi),
                               pipeline_mode=pl.Buffered(1)),
                  pl.BlockSpec((1, g, nblk, MLA_VT_ROWS, t), lambda bi, gi, i: (bi, gi, 0, 0, 0),
                               pipeline_mode=pl.Buffered(1))],
        out_specs=pl.BlockSpec((1, t, g * MLA_V), lambda bi, gi, i: (bi, i, gi)),
        out_shape=jax.ShapeDtypeStruct((b, lp, MLA_HEADS * MLA_V), BF16),
        scratch_shapes=[pltpu.VMEM((g, 1, t), F32), pltpu.VMEM((g, MLA_VT_ROWS, t), F32)],
        compiler_params=_params(("parallel", "parallel", "arbitrary")),
        name="mla_attn",
    )(qt, k, vt)


def _sc_pre_kernel(h_ref, g_ref, w_ref, cw_ref, o_ref, xs_ref, *, tm):
    d = D_MODEL

    @pl.when(pl.program_id(1) == 0)
    def _():
        xs_ref[pl.ds(tm, HALO), :] = jnp.zeros((HALO, d), F32)

    hn = _rms(h_ref[0], g_ref[...]).astype(BF16)
    gate_c = jnp.dot(hn, w_ref[:, d:2 * d], preferred_element_type=F32)
    u = jnp.dot(hn, w_ref[:, 2 * d:], preferred_element_type=F32)
    z = gate_c * u
    xs_ref[pl.ds(0, HALO), :] = xs_ref[pl.ds(tm, HALO), :]
    xs_ref[pl.ds(HALO, tm), :] = z
    y = _causal_conv(xs_ref, z, cw_ref[...], tm)
    gate_b = jnp.dot(hn, w_ref[:, :d], preferred_element_type=F32)
    o_ref[0] = (gate_b * y).astype(BF16)


def _sc_pre(h, g, w, cw, tm):
    b, lp, d = h.shape
    tok = pl.BlockSpec((1, tm, d), lambda bi, i: (bi, i, 0))
    return pl.pallas_call(
        functools.partial(_sc_pre_kernel, tm=tm),
        grid=(b, lp // tm),
        in_specs=[tok, _const_spec(g.shape), _const_spec(w.shape), _const_spec(cw.shape)],
        out_specs=tok,
        out_shape=jax.ShapeDtypeStruct((b, lp, d), BF16),
        scratch_shapes=[pltpu.VMEM((tm + HALO, d), F32)],
        compiler_params=_params(("parallel", "arbitrary")),
        name="sc_pre",
    )(h, g, w, cw)


def _diff_pre_kernel(h_ref, g_ref, wqt_ref, wk_ref, wvt_ref, qt_ref, k_ref, vt_ref):
    hn = _rms(h_ref[0], g_ref[...]).astype(BF16)
    k_ref[0] = jnp.dot(hn, wk_ref[...], preferred_element_type=F32).astype(BF16)
    qt = _dot_nt(wqt_ref[...], hn) * (DIFF_HEAD_DIM ** -0.5 * LOG2E)
    for hd in range(DIFF_HEADS):
        qt_ref[0, hd] = qt[hd * HEAD_LANES:(hd + 1) * HEAD_LANES].astype(BF16)
    _store_vt(vt_ref, _dot_nt(wvt_ref[...], hn), DIFF_HEADS, 2 * DIFF_HEAD_DIM)


def _diff_pre(h, g, wqt, wk, wvt):
    b, lp, d = h.shape
    t = ATTN_TILE
    nblk = lp // t
    tok = pl.BlockSpec((1, t, d), lambda bi, i: (bi, i, 0))
    return pl.pallas_call(
        _diff_pre_kernel,
        grid=(b, nblk),
        in_specs=[tok, _const_spec(g.shape), _const_spec(wqt.shape), _const_spec(wk.shape),
                  _const_spec(wvt.shape)],
        out_specs=[pl.BlockSpec((1, DIFF_HEADS, HEAD_LANES, t), lambda bi, i: (bi, 0, 0, i)), tok,
                   pl.BlockSpec((1, DIFF_HEADS, 1, DIFF_VT_ROWS, t), lambda bi, i: (bi, 0, i, 0, 0))],
        out_shape=[jax.ShapeDtypeStruct((b, DIFF_HEADS, HEAD_LANES, lp), BF16),
                   jax.ShapeDtypeStruct((b, lp, d), BF16),
                   jax.ShapeDtypeStruct((b, DIFF_HEADS, nblk, DIFF_VT_ROWS, t), BF16)],
        compiler_params=_params(("parallel", "parallel")),
        name="diff_pre",
    )(h, g, wqt, wk, wvt)


def _diff_attn_kernel(lam_ref, slope_ref, gsub_ref, qt_ref, k_ref, vt_ref, o_ref,
                      qs_ref, bias_ref, m_ref, acc_ref, *, lambda_init):
    t = ATTN_TILE
    dv = 2 * DIFF_HEAD_DIM
    i = pl.program_id(2)
    first_map = lax.broadcasted_iota(jnp.int32, (HEAD_LANES, t), 0) < DIFF_HEAD_DIM
    key = lax.broadcasted_iota(jnp.int32, (t, t), 0)
    query = lax.broadcasted_iota(jnp.int32, (t, t), 1)
    in_tile_dist = (key - query).astype(F32)
    for e in range(DIFF_GROUP):
        q = qt_ref[0, e]
        zero = jnp.zeros_like(q)
        qs_ref[e, :, pl.ds(0, t)] = jnp.where(first_map, q, zero)
        qs_ref[e, :, pl.ds(t, t)] = jnp.where(first_map, zero, q)
        bias_ref[e] = in_tile_dist * slope_ref[e][:, 0:1]
    m_ref[...] = jnp.full(m_ref.shape, NEG_BIG, F32)
    acc_ref[...] = jnp.zeros(acc_ref.shape, F32)

    def step(tiles):
        g = DIFF_GROUP

        def logits(v):
            (j, masked), e = tiles[v // g], v % g
            k_blk = k_ref[0, pl.ds(pl.multiple_of(j * t, t), t), e * HEAD_LANES:(e + 1) * HEAD_LANES]
            s = jnp.dot(k_blk, qs_ref[e], preferred_element_type=F32)
            bias = bias_ref[e]
            maps = [s[:, :t] + bias, s[:, t:] + bias]
            if masked:
                maps = [jnp.where(key <= query, sm, NEG_BIG) for sm in maps]
            return jnp.concatenate(maps, axis=1)

        def softmax(v, s):
            j, e = tiles[v // g][0], v % g
            tile_bias = slope_ref[e][:, 0:1] * ((j - i) * t).astype(F32)
            return _softmax_tile(s, m_ref, e, tile_bias)

        _pipelined_heads(
            g * len(tiles), logits, softmax,
            lambda v, p, alpha: _accumulate(vt_ref[0, v % g, tiles[v // g][0]], p, alpha, acc_ref, v % g))

    _sweep_key_tiles(i, step)

    lv = lam_ref[...]
    lam = (jnp.exp(jnp.sum(lv[0:1] * lv[1:2], axis=-1, keepdims=True))
           - jnp.exp(jnp.sum(lv[2:3] * lv[3:4], axis=-1, keepdims=True)) + lambda_init)
    for e in range(DIFF_GROUP):
        o_t = acc_ref[e, pl.ds(0, dv), :] / acc_ref[e, pl.ds(dv, 1), :]
        o = (o_t[:, :t] - lam * o_t[:, t:]).T
        o_ref[0, :, e * dv:(e + 1) * dv] = (_rms(o, gsub_ref[...]) * (1.0 - lambda_init)).astype(BF16)


def _diff_attn(qt, k, vt, lam_vecs, slopes, gsub, lambda_init):
    b, lp, _ = k.shape
    t = ATTN_TILE
    g = DIFF_GROUP
    nblk = lp // t
    return pl.pallas_call(
        functools.partial(_diff_attn_kernel, lambda_init=lambda_init),
        grid=(b, DIFF_HEADS // g, nblk),
        in_specs=[_const_spec(lam_vecs.shape),
                  pl.BlockSpec((g, 1, HEAD_LANES), lambda bi, gi, i: (gi, 0, 0)),
                  _const_spec(gsub.shape),
                  pl.BlockSpec((1, g, HEAD_LANES, t), lambda bi, gi, i: (bi, gi, 0, i)),
                  pl.BlockSpec((1, lp, g * HEAD_LANES), lambda bi, gi, i: (bi, 0, gi),
                               pipeline_mode=pl.Buffered(1)),
                  pl.BlockSpec((1, g, nblk, DIFF_VT_ROWS, t), lambda bi, gi, i: (bi, gi, 0, 0, 0),
                               pipeline_mode=pl.Buffered(1))],
        out_specs=pl.BlockSpec((1, t, g * HEAD_LANES), lambda bi, gi, i: (bi, i, gi)),
        out_shape=jax.ShapeDtypeStruct((b, lp, DIFF_HEADS * HEAD_LANES), BF16),
        scratch_shapes=[pltpu.VMEM((g, HEAD_LANES, 2 * t), BF16), pltpu.VMEM((g, t, t), F32),
                        pltpu.VMEM((g, 1, 2 * t), F32), pltpu.VMEM((g, DIFF_VT_ROWS, 2 * t), F32)],
        compiler_params=_params(("parallel", "parallel", "arbitrary")),
        name="diff_attn",
    )(lam_vecs, slopes, gsub, qt, k, vt)


SUBLANES = 8
LANES = 128
FFN_HALO = 2 * SUBLANES


def _interleaved_rows(tm):
    return tm // SUBLANES


def _ffn_chunks():
    return [(lo, min(FFN_CHUNK, D_FF - lo)) for lo in range(0, D_FF, FFN_CHUNK)]


def _store_lane_tiles(ref, x):
    for l in range(ref.shape[0]):
        ref[l] = x[:, l * LANES:(l + 1) * LANES]


def _strided_rows(ref, start, n, stride):
    return jnp.concatenate([ref[l, pl.ds(start, n, stride=stride), :] for l in range(ref.shape[0])], axis=1)


def _post_kernel(h_ref, o_ref, wo_ref, g_ref, wup_ref, cw_ref, wdn_ref, out_ref,
                 nat_ref, hn_ref, xs_ref, carry_ref, f_ref, *, tm):
    chunks = _ffn_chunks()
    run = _interleaved_rows(tm)

    @pl.when(pl.program_id(1) == 0)
    def _():
        carry_ref[...] = jnp.zeros(carry_ref.shape, F32)

    g = g_ref[...]
    m = jnp.dot(o_ref[0], wo_ref[...], preferred_element_type=F32)
    h1 = h_ref[0] + _rms(m, g[1:2])
    out_ref[0] = h1
    _store_lane_tiles(nat_ref, _rms(h1, g[2:3]))
    for r in range(0, run, 2):
        pair = [_strided_rows(nat_ref, r + k, SUBLANES, run) for k in range(2)]
        hn_ref[pl.ds(r * SUBLANES, 2 * SUBLANES), :] = jnp.concatenate(pair, axis=0).astype(BF16)
    f_ref[...] = jnp.zeros(f_ref.shape, F32)

    def up_proj(c):
        lo, w = chunks[c]
        hn = hn_ref[...]
        gate = jnp.dot(hn, wup_ref[:, lo:lo + w], preferred_element_type=F32)
        value = jnp.dot(hn, wup_ref[:, D_FF + lo:D_FF + lo + w], preferred_element_type=F32)
        return gate, value

    def conv_act(c, up):
        lo, w = chunks[c]
        xs = xs_ref.at[c % 2]
        cols = pl.ds(0, 2 * w)
        xs[pl.ds(FFN_HALO, tm), pl.ds(0, w)] = up[0]
        xs[pl.ds(FFN_HALO, tm), pl.ds(w, w)] = up[1]
        first_sublane = lax.broadcasted_iota(jnp.int32, (SUBLANES, 2 * w), 0) == 0
        for k in range(2):
            cur = pltpu.roll(xs[pl.ds(tm + k * SUBLANES, SUBLANES), cols], 1, 0)
            prev = pltpu.roll(carry_ref[c, pl.ds(k * SUBLANES, SUBLANES), cols], 1, 0)
            xs[pl.ds(k * SUBLANES, SUBLANES), cols] = jnp.where(first_sublane, prev, cur)
        carry_ref[c, :, cols] = xs[pl.ds(tm, FFN_HALO), cols]
        cw = jnp.concatenate([cw_ref[:, lo:lo + w], cw_ref[:, D_FF + lo:D_FF + lo + w]], axis=1)
        y = (cw[2:3] * xs[pl.ds(FFN_HALO, tm), cols] + cw[1:2] * xs[pl.ds(SUBLANES, tm), cols]
             + cw[0:1] * xs[pl.ds(0, tm), cols])
        half_gate = 0.5 * y[:, :w]
        silu = half_gate + half_gate * jnp.tanh(half_gate)
        return (silu * y[:, w:]).astype(BF16)

    nch = len(chunks)
    up_next = up_proj(0)
    for c in range(nch):
        up = up_next
        if c + 1 < nch:
            up_next = up_proj(c + 1)
        act = conv_act(c, up)
        lo, w = chunks[c]
        f_ref[...] += jnp.dot(act, wdn_ref[lo:lo + w, :], preferred_element_type=F32)
    _store_lane_tiles(nat_ref, _rms(f_ref[...], g[3:4]))
    for j in range(tm // SUBLANES):
        pieces, t = [], j * SUBLANES
        while t < (j + 1) * SUBLANES:
            s_, r_ = divmod(t, run)
            n = min((j + 1) * SUBLANES - t, run - r_)
            pieces.append(_strided_rows(nat_ref, r_ * SUBLANES + s_, n, SUBLANES))
            t += n
        rows = pl.ds(j * SUBLANES, SUBLANES)
        out_ref[0, rows, :] = out_ref[0, rows, :] + (pieces[0] if len(pieces) == 1 else jnp.concatenate(pieces, axis=0))


def _post(h, o, wo, g, wup, cw, wdn, tm):
    b, lp, d = h.shape
    nch = len(_ffn_chunks())
    tok = pl.BlockSpec((1, tm, d), lambda bi, i: (bi, i, 0))
    return pl.pallas_call(
        functools.partial(_post_kernel, tm=tm),
        grid=(b, lp // tm),
        in_specs=[tok, tok, _const_spec(wo.shape), _const_spec(g.shape), _const_spec(wup.shape),
                  _const_spec(cw.shape), _const_spec(wdn.shape)],
        out_specs=tok,
        out_shape=jax.ShapeDtypeStruct((b, lp, d), F32),
        scratch_shapes=[pltpu.VMEM((d // LANES, tm, LANES), F32), pltpu.VMEM((tm, d), BF16),
                        pltpu.VMEM((2, tm + FFN_HALO, 2 * FFN_CHUNK), F32),
                        pltpu.VMEM((nch, FFN_HALO, 2 * FFN_CHUNK), F32), pltpu.VMEM((tm, d), F32)],
        compiler_params=_params(("parallel", "arbitrary")),
        name="post_ffn",
    )(h, o, wo, g, wup, cw, wdn)


def _mla_weights(w_in, g_q, g_kv, w_uq, w_ukv):
    qr, kvr, r2 = MLA_Q_RANK, MLA_KV_RANK, MLA_ROPE // 2
    kr = w_in[:, qr + kvr:]
    pad = lambda a, lo: jnp.pad(a, ((0, 0), (lo, HEAD_LANES - lo - a.shape[1])))
    kr_swapped = jnp.concatenate([kr[:, r2:], kr[:, :r2]], axis=1)
    win = jnp.concatenate([w_in[:, :qr + kvr], pad(kr, MLA_NOPE), pad(kr_swapped, MLA_NOPE)], axis=1)
    wq = w_uq.reshape(qr, MLA_HEADS, MLA_QK)
    wqa = jnp.pad(wq, ((0, 0), (0, 0), (0, HEAD_LANES - MLA_QK)))
    rope_swapped = jnp.concatenate([wq[..., MLA_NOPE + r2:], wq[..., MLA_NOPE:MLA_NOPE + r2]], axis=-1)
    wqb = jnp.pad(rope_swapped, ((0, 0), (0, 0), (MLA_NOPE, HEAD_LANES - MLA_QK)))
    wkv = w_ukv.reshape(kvr, MLA_HEADS, MLA_NOPE + MLA_V)
    wk = jnp.pad(wkv[..., :MLA_NOPE], ((0, 0), (0, 0), (0, HEAD_LANES - MLA_NOPE)))
    wv = wkv[..., MLA_NOPE:]
    flat = lambda a: a.reshape(a.shape[0], -1).astype(BF16)
    return dict(win=win.astype(BF16), gq=g_q[None], gkv=g_kv[None], wqa=flat(wqa).T, wqb=flat(wqb).T,
                wk=flat(wk), wv=flat(wv).T)


def _rope_tables(lp):
    inv_freq = ROPE_THETA ** (-jnp.arange(0, MLA_ROPE, 2, dtype=F32) / MLA_ROPE)
    ang = jnp.arange(lp, dtype=F32)[:, None] * inv_freq[None, :]
    cos, sin = jnp.cos(ang), jnp.sin(ang)
    lay = lambda nope, a, b_: jnp.concatenate(
        [jnp.full((lp, MLA_NOPE), nope, F32), a, b_, jnp.zeros((lp, HEAD_LANES - MLA_QK), F32)], axis=1)
    scale = MLA_QK ** -0.5 * LOG2E
    return ((lay(1.0, cos, cos) * scale).T, (lay(0.0, -sin, sin) * scale).T,
            lay(0.0, cos, cos), lay(0.0, -sin, sin))


def kernel(x, meta_tokens, norms, mla_w_in, mla_norm_q, mla_norm_kv, mla_w_uq, mla_w_ukv, mla_w_o, sc_w_in, sc_conv, sc_w_out, diff_w_in, diff_lambda_q1, diff_lambda_k1, diff_lambda_q2, diff_lambda_k2, diff_subln, diff_w_o, ffn_w_up, ffn_conv, ffn_w_down):
    b, seq, d = x.shape
    depth = norms.shape[0]
    length = N_META + seq
    lp = -(-length // ATTN_TILE) * ATTN_TILE
    tm = _token_tile(lp)
    meta = jnp.broadcast_to(meta_tokens[None].astype(x.dtype), (b, N_META, d))
    h = jnp.concatenate([meta, x, jnp.zeros((b, lp - length, d), x.dtype)], axis=1)
    tabs = _rope_tables(lp)
    slopes = 2.0 ** (-8.0 * jnp.arange(1, DIFF_HEADS + 1, dtype=F32) / DIFF_HEADS) * LOG2E
    slopes = jnp.broadcast_to(slopes[:, None, None], (DIFF_HEADS, 1, HEAD_LANES))

    for i in range(depth):
        kind, j = i % N_MIXERS, i // N_MIXERS
        g = norms[i]
        if kind == 0:
            w = _mla_weights(mla_w_in[j], mla_norm_q[j], mla_norm_kv[j], mla_w_uq[j], mla_w_ukv[j])
            o = _mla_attn(*_mla_pre(h, g[0:1], w, tabs))
            wo = mla_w_o[j]
        elif kind == 1:
            o = _sc_pre(h, g[0:1], sc_w_in[j].astype(BF16), sc_conv[j], tm)
            wo = sc_w_out[j]
        else:
            lambda_init = 0.8 - 0.6 * math.exp(-0.3 * i)
            w_in = diff_w_in[j].astype(BF16)
            qt, k, vt = _diff_pre(h, g[0:1], w_in[:, :d].T, w_in[:, d:2 * d], w_in[:, 2 * d:].T)
            lam_vecs = jnp.stack([diff_lambda_q1[j], diff_lambda_k1[j], diff_lambda_q2[j], diff_lambda_k2[j]])
            o = _diff_attn(qt, k, vt, lam_vecs, slopes, diff_subln[j][None], lambda_init)
            wo = diff_w_o[j]
        h = _post(h, o, wo.astype(BF16), g, ffn_w_up[i].astype(BF16), ffn_conv[i],
                  ffn_w_down[i].astype(BF16), tm)
    return h[:, N_META:length]
```

```python
import functools
import math

import jax
import jax.numpy as jnp
from jax import lax
from jax.experimental import pallas as pl
from jax.experimental.pallas import tpu as pltpu

F32 = jnp.float32
BF16 = jnp.bfloat16

D_MODEL = 1024
N_META = 16
EPS = 1e-6
N_MIXERS = 3
LOG2E = math.log2(math.e)

MLA_HEADS = 16
MLA_Q_RANK = 256
MLA_KV_RANK = 128
MLA_NOPE = 64
MLA_ROPE = 32
MLA_V = 64
MLA_QK = MLA_NOPE + MLA_ROPE
ROPE_THETA = 10000.0
HEAD_LANES = 128
ONES_ROWS = 16
MLA_GROUP = 16
MLA_VT_ROWS = MLA_V + ONES_ROWS

DIFF_HEADS = 8
DIFF_HEAD_DIM = 64
DIFF_GROUP = 8
DIFF_VT_ROWS = 2 * DIFF_HEAD_DIM + ONES_ROWS

D_FF = 2816
FFN_CHUNK = 768
KEY_TILES_PER_TRIP = 4
HALO = 8

ATTN_TILE = 256
LOGITS_AHEAD = 4
ACCUMULATE_BEHIND = 2
TOKEN_TILE_TARGET = 544
NEG_BIG = -1e30
VMEM_LIMIT = 52 * 1024 * 1024


def _rms(x, g):
    return x * lax.rsqrt(jnp.mean(x * x, axis=-1, keepdims=True) + EPS) * g


def _dot_nt(a, b):
    return lax.dot_general(a, b, (((1,), (1,)), ((), ())), preferred_element_type=F32)


def _token_tile(lp):
    best = 16
    for t in range(16, min(lp, TOKEN_TILE_TARGET) + 1, 16):
        if lp % t == 0:
            best = t
    return best


def _const_spec(shape):
    nd = len(shape)
    return pl.BlockSpec(shape, lambda *_: (0,) * nd, pipeline_mode=pl.Buffered(1))


def _params(sem):
    return pltpu.CompilerParams(dimension_semantics=sem, vmem_limit_bytes=VMEM_LIMIT)


def _causal_conv(xs_ref, cur, cw, tm):
    return (cw[2:3] * cur + cw[1:2] * xs_ref[pl.ds(HALO - 1, tm), :]
            + cw[0:1] * xs_ref[pl.ds(HALO - 2, tm), :])


def _store_vt(vt_ref, vt, heads, width):
    t = vt.shape[1]
    for hd in range(heads):
        vt_ref[0, hd, 0, pl.ds(0, width), :] = vt[hd * width:(hd + 1) * width].astype(BF16)
        vt_ref[0, hd, 0, pl.ds(width, ONES_ROWS), :] = jnp.ones((ONES_ROWS, t), BF16)


def _softmax_tile(s, m_ref, e, tile_bias):
    m_old = m_ref[e]
    m_blk = jnp.max(s, axis=0, keepdims=True)
    if tile_bias is not None:
        m_blk = m_blk + tile_bias
    m_new = jnp.maximum(m_old, m_blk)
    m_ref[e] = m_new
    shift = m_new if tile_bias is None else m_new - tile_bias
    return jnp.exp2(s - shift).astype(BF16), jnp.exp2(m_old - m_new)


def _accumulate(vt_blk, p, alpha, acc_ref, e):
    pv = jnp.dot(vt_blk, p, preferred_element_type=F32)
    acc_ref[e] = alpha * acc_ref[e] + pv


def _pipelined_heads(n, logits, softmax, accumulate):
    ahead = [logits(e) for e in range(min(LOGITS_AHEAD, n))]
    pending = []
    for e in range(n):
        s = ahead.pop(0)
        if e + LOGITS_AHEAD < n:
            ahead.append(logits(e + LOGITS_AHEAD))
        p, alpha = softmax(e, s)
        pending.append((e, p, alpha))
        if len(pending) > ACCUMULATE_BEHIND:
            accumulate(*pending.pop(0))
    for item in pending:
        accumulate(*item)


def _sweep_key_tiles(i, step):
    n = KEY_TILES_PER_TRIP

    def body(trip, carry):
        step([(n * trip + k, False) for k in range(n)])
        return carry

    lax.fori_loop(0, i // n, body, 0)
    for left in range(n):
        @pl.when(i % n == left)
        def _():
            step([(i - left + k, False) for k in range(left)] + [(i, True)])


def _causal_tile_mask(t):
    key = lax.broadcasted_iota(jnp.int32, (t, t), 0)
    query = lax.broadcasted_iota(jnp.int32, (t, t), 1)
    return key <= query


def _mla_pre_kernel(h_ref, g0_ref, win_ref, gq_ref, gkv_ref, wqa_ref, wqb_ref, wk_ref, wv_ref,
                    cq_ref, sq_ref, ck_ref, sk_ref, qt_ref, k_ref, vt_ref):
    hn = _rms(h_ref[0], g0_ref[...]).astype(BF16)
    c = jnp.dot(hn, win_ref[...], preferred_element_type=F32)
    cq = _rms(c[:, :MLA_Q_RANK], gq_ref[...]).astype(BF16)
    ckv = _rms(c[:, MLA_Q_RANK:MLA_Q_RANK + MLA_KV_RANK], gkv_ref[...]).astype(BF16)
    o = MLA_Q_RANK + MLA_KV_RANK
    k_rope = (c[:, o:o + HEAD_LANES] * ck_ref[...]
              + c[:, o + HEAD_LANES:o + 2 * HEAD_LANES] * sk_ref[...])
    ka = jnp.dot(ckv, wk_ref[...], preferred_element_type=F32)
    for hd in range(MLA_HEADS):
        d = slice(hd * HEAD_LANES, (hd + 1) * HEAD_LANES)
        k_ref[0, :, d] = (ka[:, d] + k_rope).astype(BF16)
    cq_t = cq_ref[...]
    sq_t = sq_ref[...]
    per = 4
    for c0 in range(0, MLA_HEADS, per):
        rows = slice(c0 * HEAD_LANES, (c0 + per) * HEAD_LANES)
        qa = _dot_nt(wqa_ref[rows, :], cq)
        qb = _dot_nt(wqb_ref[rows, :], cq)
        for e in range(per):
            a = slice(e * HEAD_LANES, (e + 1) * HEAD_LANES)
            qt_ref[0, c0 + e] = (qa[a] * cq_t + qb[a] * sq_t).astype(BF16)
    _store_vt(vt_ref, _dot_nt(wv_ref[...], ckv), MLA_HEADS, MLA_V)


def _mla_pre(h, g0, w, tabs):
    b, lp, d = h.shape
    t = ATTN_TILE
    nblk = lp // t
    tok = lambda n: pl.BlockSpec((1, t, n), lambda bi, i: (bi, i, 0))
    tab = pl.BlockSpec((t, HEAD_LANES), lambda bi, i: (i, 0))
    tab_t = pl.BlockSpec((HEAD_LANES, t), lambda bi, i: (0, i))
    return pl.pallas_call(
        _mla_pre_kernel,
        grid=(b, nblk),
        in_specs=[tok(d), _const_spec(g0.shape), _const_spec(w["win"].shape),
                  _const_spec(w["gq"].shape), _const_spec(w["gkv"].shape),
                  _const_spec(w["wqa"].shape), _const_spec(w["wqb"].shape),
                  _const_spec(w["wk"].shape), _const_spec(w["wv"].shape), tab_t, tab_t, tab, tab],
        out_specs=[pl.BlockSpec((1, MLA_HEADS, HEAD_LANES, t), lambda bi, i: (bi, 0, 0, i)),
                   tok(MLA_HEADS * HEAD_LANES),
                   pl.BlockSpec((1, MLA_HEADS, 1, MLA_VT_ROWS, t), lambda bi, i: (bi, 0, i, 0, 0))],
        out_shape=[jax.ShapeDtypeStruct((b, MLA_HEADS, HEAD_LANES, lp), BF16),
                   jax.ShapeDtypeStruct((b, lp, MLA_HEADS * HEAD_LANES), BF16),
                   jax.ShapeDtypeStruct((b, MLA_HEADS, nblk, MLA_VT_ROWS, t), BF16)],
        compiler_params=_params(("parallel", "parallel")),
        name="mla_pre",
    )(h, g0, w["win"], w["gq"], w["gkv"], w["wqa"], w["wqb"], w["wk"], w["wv"], *tabs)


def _mla_attn_kernel(qt_ref, k_ref, vt_ref, o_ref, m_ref, acc_ref):
    t = ATTN_TILE
    i = pl.program_id(2)
    m_ref[...] = jnp.full(m_ref.shape, NEG_BIG, F32)
    acc_ref[...] = jnp.zeros(acc_ref.shape, F32)

    def step(tiles):
        g = MLA_GROUP

        def logits(v):
            (j, masked), e = tiles[v // g], v % g
            k_blk = k_ref[0, pl.ds(pl.multiple_of(j * t, t), t), e * HEAD_LANES:(e + 1) * HEAD_LANES]
            s = jnp.dot(k_blk, qt_ref[0, e], preferred_element_type=F32)
            return jnp.where(_causal_tile_mask(t), s, NEG_BIG) if masked else s

        _pipelined_heads(
            g * len(tiles), logits,
            lambda v, s: _softmax_tile(s, m_ref, v % g, None),
            lambda v, p, alpha: _accumulate(vt_ref[0, v % g, tiles[v // g][0]], p, alpha, acc_ref, v % g))

    _sweep_key_tiles(i, step)
    for p in range(MLA_GROUP // 2):
        halves = [acc_ref[2 * p + e, pl.ds(0, MLA_V), :] / acc_ref[2 * p + e, pl.ds(MLA_V, 1), :]
                  for e in range(2)]
        o_ref[0, :, p * HEAD_LANES:(p + 1) * HEAD_LANES] = jnp.concatenate(halves, axis=0).T.astype(BF16)


def _mla_attn(qt, k, vt):
    b, lp, _ = k.shape
    t = ATTN_TILE
    g = MLA_GROUP
    nblk = lp // t
    return pl.pallas_call(
        _mla_attn_kernel,
        grid=(b, MLA_HEADS // g, nblk),
        in_specs=[pl.BlockSpec((1, g, HEAD_LANES, t), lambda bi, gi, i: (bi, gi, 0, i)),
                  pl.BlockSpec((1, lp, g * HEAD_LANES), lambda bi, gi, i: (bi, 0, gi),
                               pipeline_mode=pl.Buffered(1)),
                  pl.BlockSpec((1, g, nblk, MLA_VT_ROWS, t), lambda bi, gi, i: (bi, gi, 0, 0, 0),
                               pipeline_mode=pl.Buffered(1))],
        out_specs=pl.BlockSpec((1, t, g * MLA_V), lambda bi, gi, i: (bi, i, gi)),
        out_shape=jax.ShapeDtypeStruct((b, lp, MLA_HEADS * MLA_V), BF16),
        scratch_shapes=[pltpu.VMEM((g, 1, t), F32), pltpu.VMEM((g, MLA_VT_ROWS, t), F32)],
        compiler_params=_params(("parallel", "parallel", "arbitrary")),
        name="mla_attn",
    )(qt, k, vt)


def _sc_pre_kernel(h_ref, g_ref, w_ref, cw_ref, o_ref, xs_ref, *, tm):
    d = D_MODEL

    @pl.when(pl.program_id(1) == 0)
    def _():
        xs_ref[pl.ds(tm, HALO), :] = jnp.zeros((HALO, d), F32)

    hn = _rms(h_ref[0], g_ref[...]).astype(BF16)
    gate_c = jnp.dot(hn, w_ref[:, d:2 * d], preferred_element_type=F32)
    u = jnp.dot(hn, w_ref[:, 2 * d:], preferred_element_type=F32)
    z = gate_c * u
    xs_ref[pl.ds(0, HALO), :] = xs_ref[pl.ds(tm, HALO), :]
    xs_ref[pl.ds(HALO, tm), :] = z
    y = _causal_conv(xs_ref, z, cw_ref[...], tm)
    gate_b = jnp.dot(hn, w_ref[:, :d], preferred_element_type=F32)
    o_ref[0] = (gate_b * y).astype(BF16)


def _sc_pre(h, g, w, cw, tm):
    b, lp, d = h.shape
    tok = pl.BlockSpec((1, tm, d), lambda bi, i: (bi, i, 0))
    return pl.pallas_call(
        functools.partial(_sc_pre_kernel, tm=tm),
        grid=(b, lp // tm),
        in_specs=[tok, _const_spec(g.shape), _const_spec(w.shape), _const_spec(cw.shape)],
        out_specs=tok,
        out_shape=jax.ShapeDtypeStruct((b, lp, d), BF16),
        scratch_shapes=[pltpu.VMEM((tm + HALO, d), F32)],
        compiler_params=_params(("parallel", "arbitrary")),
        name="sc_pre",
    )(h, g, w, cw)


def _diff_pre_kernel(h_ref, g_ref, wqt_ref, wk_ref, wvt_ref, qt_ref, k_ref, vt_ref):
    hn = _rms(h_ref[0], g_ref[...]).astype(BF16)
    k_ref[0] = jnp.dot(hn, wk_ref[...], preferred_element_type=F32).astype(BF16)
    qt = _dot_nt(wqt_ref[...], hn) * (DIFF_HEAD_DIM ** -0.5 * LOG2E)
    for hd in range(DIFF_HEADS):
        qt_ref[0, hd] = qt[hd * HEAD_LANES:(hd + 1) * HEAD_LANES].astype(BF16)
    _store_vt(vt_ref, _dot_nt(wvt_ref[...], hn), DIFF_HEADS, 2 * DIFF_HEAD_DIM)


def _diff_pre(h, g, wqt, wk, wvt):
    b, lp, d = h.shape
    t = ATTN_TILE
    nblk = lp // t
    tok = pl.BlockSpec((1, t, d), lambda bi, i: (bi, i, 0))
    return pl.pallas_call(
        _diff_pre_kernel,
        grid=(b, nblk),
        in_specs=[tok, _const_spec(g.shape), _const_spec(wqt.shape), _const_spec(wk.shape),
                  _const_spec(wvt.shape)],
        out_specs=[pl.BlockSpec((1, DIFF_HEADS, HEAD_LANES, t), lambda bi, i: (bi, 0, 0, i)), tok,
                   pl.BlockSpec((1, DIFF_HEADS, 1, DIFF_VT_ROWS, t), lambda bi, i: (bi, 0, i, 0, 0))],
        out_shape=[jax.ShapeDtypeStruct((b, DIFF_HEADS, HEAD_LANES, lp), BF16),
                   jax.ShapeDtypeStruct((b, lp, d), BF16),
                   jax.ShapeDtypeStruct((b, DIFF_HEADS, nblk, DIFF_VT_ROWS, t), BF16)],
        compiler_params=_params(("parallel", "parallel")),
        name="diff_pre",
    )(h, g, wqt, wk, wvt)


def _diff_attn_kernel(lam_ref, slope_ref, gsub_ref, qt_ref, k_ref, vt_ref, o_ref,
                      qs_ref, bias_ref, m_ref, acc_ref, *, lambda_init):
    t = ATTN_TILE
    dv = 2 * DIFF_HEAD_DIM
    i = pl.program_id(2)
    first_map = lax.broadcasted_iota(jnp.int32, (HEAD_LANES, t), 0) < DIFF_HEAD_DIM
    key = lax.broadcasted_iota(jnp.int32, (t, t), 0)
    query = lax.broadcasted_iota(jnp.int32, (t, t), 1)
    in_tile_dist = (key - query).astype(F32)
    for e in range(DIFF_GROUP):
        q = qt_ref[0, e]
        zero = jnp.zeros_like(q)
        qs_ref[e, :, pl.ds(0, t)] = jnp.where(first_map, q, zero)
        qs_ref[e, :, pl.ds(t, t)] = jnp.where(first_map, zero, q)
        bias_ref[e] = in_tile_dist * slope_ref[e][:, 0:1]
    m_ref[...] = jnp.full(m_ref.shape, NEG_BIG, F32)
    acc_ref[...] = jnp.zeros(acc_ref.shape, F32)

    def step(tiles):
        g = DIFF_GROUP

        def logits(v):
            (j, masked), e = tiles[v // g], v % g
            k_blk = k_ref[0, pl.ds(pl.multiple_of(j * t, t), t), e * HEAD_LANES:(e + 1) * HEAD_LANES]
            s = jnp.dot(k_blk, qs_ref[e], preferred_element_type=F32)
            bias = bias_ref[e]
            maps = [s[:, :t] + bias, s[:, t:] + bias]
            if masked:
                maps = [jnp.where(key <= query, sm, NEG_BIG) for sm in maps]
            return jnp.concatenate(maps, axis=1)

        def softmax(v, s):
            j, e = tiles[v // g][0], v % g
            tile_bias = slope_ref[e][:, 0:1] * ((j - i) * t).astype(F32)
            return _softmax_tile(s, m_ref, e, tile_bias)

        _pipelined_heads(
            g * len(tiles), logits, softmax,
            lambda v, p, alpha: _accumulate(vt_ref[0, v % g, tiles[v // g][0]], p, alpha, acc_ref, v % g))

    _sweep_key_tiles(i, step)

    lv = lam_ref[...]
    lam = (jnp.exp(jnp.sum(lv[0:1] * lv[1:2], axis=-1, keepdims=True))
           - jnp.exp(jnp.sum(lv[2:3] * lv[3:4], axis=-1, keepdims=True)) + lambda_init)
    for e in range(DIFF_GROUP):
        o_t = acc_ref[e, pl.ds(0, dv), :] / acc_ref[e, pl.ds(dv, 1), :]
        o = (o_t[:, :t] - lam * o_t[:, t:]).T
        o_ref[0, :, e * dv:(e + 1) * dv] = (_rms(o, gsub_ref[...]) * (1.0 - lambda_init)).astype(BF16)


def _diff_attn(qt, k, vt, lam_vecs, slopes, gsub, lambda_init):
    b, lp, _ = k.shape
    t = ATTN_TILE
    g = DIFF_GROUP
    nblk = lp // t
    return pl.pallas_call(
        functools.partial(_diff_attn_kernel, lambda_init=lambda_init),
        grid=(b, DIFF_HEADS // g, nblk),
        in_specs=[_const_spec(lam_vecs.shape),
                  pl.BlockSpec((g, 1, HEAD_LANES), lambda bi, gi, i: (gi, 0, 0)),
                  _const_spec(gsub.shape),
                  pl.BlockSpec((1, g, HEAD_LANES, t), lambda bi, gi, i: (bi, gi, 0, i)),
                  pl.BlockSpec((1, lp, g * HEAD_LANES), lambda bi, gi, i: (bi, 0, gi)),
                  pl.BlockSpec((1, g, nblk, DIFF_VT_ROWS, t), lambda bi, gi, i: (bi, gi, 0, 0, 0))],
        out_specs=pl.BlockSpec((1, t, g * HEAD_LANES), lambda bi, gi, i: (bi, i, gi)),
        out_shape=jax.ShapeDtypeStruct((b, lp, DIFF_HEADS * HEAD_LANES), BF16),
        scratch_shapes=[pltpu.VMEM((g, HEAD_LANES, 2 * t), BF16), pltpu.VMEM((g, t, t), F32),
                        pltpu.VMEM((g, 1, 2 * t), F32), pltpu.VMEM((g, DIFF_VT_ROWS, 2 * t), F32)],
        compiler_params=_params(("parallel", "parallel", "arbitrary")),
        name="diff_attn",
    )(lam_vecs, slopes, gsub, qt, k, vt)


SUBLANES = 8
LANES = 128
FFN_HALO = 2 * SUBLANES


def _interleaved_rows(tm):
    return tm // SUBLANES


def _ffn_chunks():
    return [(lo, min(FFN_CHUNK, D_FF - lo)) for lo in range(0, D_FF, FFN_CHUNK)]


def _store_lane_tiles(ref, x):
    for l in range(ref.shape[0]):
        ref[l] = x[:, l * LANES:(l + 1) * LANES]


def _strided_rows(ref, start, n, stride):
    return jnp.concatenate([ref[l, pl.ds(start, n, stride=stride), :] for l in range(ref.shape[0])], axis=1)


def _post_kernel(h_ref, o_ref, wo_ref, g_ref, wup_ref, cw_ref, wdn_ref, out_ref,
                 nat_ref, hn_ref, xs_ref, carry_ref, f_ref, *, tm):
    chunks = _ffn_chunks()
    run = _interleaved_rows(tm)

    @pl.when(pl.program_id(1) == 0)
    def _():
        carry_ref[...] = jnp.zeros(carry_ref.shape, F32)

    g = g_ref[...]
    m = jnp.dot(o_ref[0], wo_ref[...], preferred_element_type=F32)
    h1 = h_ref[0] + _rms(m, g[1:2])
    out_ref[0] = h1
    _store_lane_tiles(nat_ref, _rms(h1, g[2:3]))
    for r in range(0, run, 2):
        pair = [_strided_rows(nat_ref, r + k, SUBLANES, run) for k in range(2)]
        hn_ref[pl.ds(r * SUBLANES, 2 * SUBLANES), :] = jnp.concatenate(pair, axis=0).astype(BF16)
    f_ref[...] = jnp.zeros(f_ref.shape, F32)

    def up_proj(c):
        lo, w = chunks[c]
        hn = hn_ref[...]
        gate = jnp.dot(hn, wup_ref[:, lo:lo + w], preferred_element_type=F32)
        value = jnp.dot(hn, wup_ref[:, D_FF + lo:D_FF + lo + w], preferred_element_type=F32)
        return gate, value

    def conv_act(c, up):
        lo, w = chunks[c]
        xs = xs_ref.at[c % 2]
        cols = pl.ds(0, 2 * w)
        xs[pl.ds(FFN_HALO, tm), pl.ds(0, w)] = up[0]
        xs[pl.ds(FFN_HALO, tm), pl.ds(w, w)] = up[1]
        first_sublane = lax.broadcasted_iota(jnp.int32, (SUBLANES, 2 * w), 0) == 0
        for k in range(2):
            cur = pltpu.roll(xs[pl.ds(tm + k * SUBLANES, SUBLANES), cols], 1, 0)
            prev = pltpu.roll(carry_ref[c, pl.ds(k * SUBLANES, SUBLANES), cols], 1, 0)
            xs[pl.ds(k * SUBLANES, SUBLANES), cols] = jnp.where(first_sublane, prev, cur)
        carry_ref[c, :, cols] = xs[pl.ds(tm, FFN_HALO), cols]
        cw = jnp.concatenate([cw_ref[:, lo:lo + w], cw_ref[:, D_FF + lo:D_FF + lo + w]], axis=1)
        y = (cw[2:3] * xs[pl.ds(FFN_HALO, tm), cols] + cw[1:2] * xs[pl.ds(SUBLANES, tm), cols]
             + cw[0:1] * xs[pl.ds(0, tm), cols])
        half_gate = 0.5 * y[:, :w]
        silu = half_gate + half_gate * jnp.tanh(half_gate)
        return (silu * y[:, w:]).astype(BF16)

    nch = len(chunks)
    up_next = up_proj(0)
    for c in range(nch):
        up = up_next
        if c + 1 < nch:
            up_next = up_proj(c + 1)
        act = conv_act(c, up)
        lo, w = chunks[c]
        f_ref[...] += jnp.dot(act, wdn_ref[lo:lo + w, :], preferred_element_type=F32)
    _store_lane_tiles(nat_ref, _rms(f_ref[...], g[3:4]))
    for j in range(tm // SUBLANES):
        pieces, t = [], j * SUBLANES
        while t < (j + 1) * SUBLANES:
            s_, r_ = divmod(t, run)
            n = min((j + 1) * SUBLANES - t, run - r_)
            pieces.append(_strided_rows(nat_ref, r_ * SUBLANES + s_, n, SUBLANES))
            t += n
        rows = pl.ds(j * SUBLANES, SUBLANES)
        out_ref[0, rows, :] = out_ref[0, rows, :] + (pieces[0] if len(pieces) == 1 else jnp.concatenate(pieces, axis=0))


def _post(h, o, wo, g, wup, cw, wdn, tm):
    b, lp, d = h.shape
    nch = len(_ffn_chunks())
    tok = pl.BlockSpec((1, tm, d), lambda bi, i: (bi, i, 0))
    return pl.pallas_call(
        functools.partial(_post_kernel, tm=tm),
        grid=(b, lp // tm),
        in_specs=[tok, tok, _const_spec(wo.shape), _const_spec(g.shape), _const_spec(wup.shape),
                  _const_spec(cw.shape), _const_spec(wdn.shape)],
        out_specs=tok,
        out_shape=jax.ShapeDtypeStruct((b, lp, d), F32),
        scratch_shapes=[pltpu.VMEM((d // LANES, tm, LANES), F32), pltpu.VMEM((tm, d), BF16),
                        pltpu.VMEM((2, tm + FFN_HALO, 2 * FFN_CHUNK), F32),
                        pltpu.VMEM((nch, FFN_HALO, 2 * FFN_CHUNK), F32), pltpu.VMEM((tm, d), F32)],
        compiler_params=_params(("parallel", "arbitrary")),
        name="post_ffn",
    )(h, o, wo, g, wup, cw, wdn)


def _mla_weights(w_in, g_q, g_kv, w_uq, w_ukv):
    qr, kvr, r2 = MLA_Q_RANK, MLA_KV_RANK, MLA_ROPE // 2
    kr = w_in[:, qr + kvr:]
    pad = lambda a, lo: jnp.pad(a, ((0, 0), (lo, HEAD_LANES - lo - a.shape[1])))
    kr_swapped = jnp.concatenate([kr[:, r2:], kr[:, :r2]], axis=1)
    win = jnp.concatenate([w_in[:, :qr + kvr], pad(kr, MLA_NOPE), pad(kr_swapped, MLA_NOPE)], axis=1)
    wq = w_uq.reshape(qr, MLA_HEADS, MLA_QK)
    wqa = jnp.pad(wq, ((0, 0), (0, 0), (0, HEAD_LANES - MLA_QK)))
    rope_swapped = jnp.concatenate([wq[..., MLA_NOPE + r2:], wq[..., MLA_NOPE:MLA_NOPE + r2]], axis=-1)
    wqb = jnp.pad(rope_swapped, ((0, 0), (0, 0), (MLA_NOPE, HEAD_LANES - MLA_QK)))
    wkv = w_ukv.reshape(kvr, MLA_HEADS, MLA_NOPE + MLA_V)
    wk = jnp.pad(wkv[..., :MLA_NOPE], ((0, 0), (0, 0), (0, HEAD_LANES - MLA_NOPE)))
    wv = wkv[..., MLA_NOPE:]
    flat = lambda a: a.reshape(a.shape[0], -1).astype(BF16)
    return dict(win=win.astype(BF16), gq=g_q[None], gkv=g_kv[None], wqa=flat(wqa).T, wqb=flat(wqb).T,
                wk=flat(wk), wv=flat(wv).T)


def _rope_tables(lp):
    inv_freq = ROPE_THETA ** (-jnp.arange(0, MLA_ROPE, 2, dtype=F32) / MLA_ROPE)
    ang = jnp.arange(lp, dtype=F32)[:, None] * inv_freq[None, :]
    cos, sin = jnp.cos(ang), jnp.sin(ang)
    lay = lambda nope, a, b_: jnp.concatenate(
        [jnp.full((lp, MLA_NOPE), nope, F32), a, b_, jnp.zeros((lp, HEAD_LANES - MLA_QK), F32)], axis=1)
    scale = MLA_QK ** -0.5 * LOG2E
    return ((lay(1.0, cos, cos) * scale).T, (lay(0.0, -sin, sin) * scale).T,
            lay(0.0, cos, cos), lay(0.0, -sin, sin))


def kernel(x, meta_tokens, norms, mla_w_in, mla_norm_q, mla_norm_kv, mla_w_uq, mla_w_ukv, mla_w_o, sc_w_in, sc_conv, sc_w_out, diff_w_in, diff_lambda_q1, diff_lambda_k1, diff_lambda_q2, diff_lambda_k2, diff_subln, diff_w_o, ffn_w_up, ffn_conv, ffn_w_down):
    b, seq, d = x.shape
    depth = norms.shape[0]
    length = N_META + seq
    lp = -(-length // ATTN_TILE) * ATTN_TILE
    tm = _token_tile(lp)
    meta = jnp.broadcast_to(meta_tokens[None].astype(x.dtype), (b, N_META, d))
    h = jnp.concatenate([meta, x, jnp.zeros((b, lp - length, d), x.dtype)], axis=1)
    tabs = _rope_tables(lp)
    slopes = 2.0 ** (-8.0 * jnp.arange(1, DIFF_HEADS + 1, dtype=F32) / DIFF_HEADS) * LOG2E
    slopes = jnp.broadcast_to(slopes[:, None, None], (DIFF_HEADS, 1, HEAD_LANES))

    for i in range(depth):
        kind, j = i % N_MIXERS, i // N_MIXERS
        g = norms[i]
        if kind == 0:
            w = _mla_weights(mla_w_in[j], mla_norm_q[j], mla_norm_kv[j], mla_w_uq[j], mla_w_ukv[j])
            o = _mla_attn(*_mla_pre(h, g[0:1], w, tabs))
            wo = mla_w_o[j]
        elif kind == 1:
            o = _sc_pre(h, g[0:1], sc_w_in[j].astype(BF16), sc_conv[j], tm)
            wo = sc_w_out[j]
        else:
            lambda_init = 0.8 - 0.6 * math.exp(-0.3 * i)
            w_in = diff_w_in[j].astype(BF16)
            qt, k, vt = _diff_pre(h, g[0:1], w_in[:, :d].T, w_in[:, d:2 * d], w_in[:, 2 * d:].T)
            lam_vecs = jnp.stack([diff_lambda_q1[j], diff_lambda_k1[j], diff_lambda_q2[j], diff_lambda_k2[j]])
            o = _diff_attn(qt, k, vt, lam_vecs, slopes, diff_subln[j][None], lambda_init)
            wo = diff_w_o[j]
        h = _post(h, o, wo.astype(BF16), g, ffn_w_up[i].astype(BF16), ffn_conv[i],
                  ffn_w_down[i].astype(BF16), tm)
    return h[:, N_META:length]
```

```python
import functools
import math

import jax
import jax.numpy as jnp
from jax import lax
from jax.experimental import pallas as pl
from jax.experimental.pallas import tpu as pltpu

F32 = jnp.float32
BF16 = jnp.bfloat16

D_MODEL = 1024
N_META = 16
EPS = 1e-6
N_MIXERS = 3
LOG2E = math.log2(math.e)

MLA_HEADS = 16
MLA_Q_RANK = 256
MLA_KV_RANK = 128
MLA_NOPE = 64
MLA_ROPE = 32
MLA_V = 64
MLA_QK = MLA_NOPE + MLA_ROPE
ROPE_THETA = 10000.0
HEAD_LANES = 128
ONES_ROWS = 16
MLA_GROUP = 8
MLA_KEY_TILES_PER_TRIP = 8
MLA_VT_ROWS = MLA_V + ONES_ROWS

DIFF_HEADS = 8
DIFF_HEAD_DIM = 64
DIFF_GROUP = 8
DIFF_VT_ROWS = 2 * DIFF_HEAD_DIM + ONES_ROWS

D_FF = 2816
FFN_CHUNK = 768
KEY_TILES_PER_TRIP = 4
HALO = 8

ATTN_TILE = 256
LOGITS_AHEAD = 4
ACCUMULATE_BEHIND = 2
TOKEN_TILE_TARGET = 544
NEG_BIG = -1e30
VMEM_LIMIT = 52 * 1024 * 1024


def _rms(x, g):
    return x * lax.rsqrt(jnp.mean(x * x, axis=-1, keepdims=True) + EPS) * g


def _dot_nt(a, b):
    return lax.dot_general(a, b, (((1,), (1,)), ((), ())), preferred_element_type=F32)


def _token_tile(lp):
    best = 16
    for t in range(16, min(lp, TOKEN_TILE_TARGET) + 1, 16):
        if lp % t == 0:
            best = t
    return best


def _const_spec(shape):
    nd = len(shape)
    return pl.BlockSpec(shape, lambda *_: (0,) * nd, pipeline_mode=pl.Buffered(1))


def _params(sem):
    return pltpu.CompilerParams(dimension_semantics=sem, vmem_limit_bytes=VMEM_LIMIT)


def _causal_conv(xs_ref, cur, cw, tm):
    return (cw[2:3] * cur + cw[1:2] * xs_ref[pl.ds(HALO - 1, tm), :]
            + cw[0:1] * xs_ref[pl.ds(HALO - 2, tm), :])


def _store_vt(vt_ref, vt, heads, width):
    t = vt.shape[1]
    for hd in range(heads):
        vt_ref[0, hd, 0, pl.ds(0, width), :] = vt[hd * width:(hd + 1) * width].astype(BF16)
        vt_ref[0, hd, 0, pl.ds(width, ONES_ROWS), :] = jnp.ones((ONES_ROWS, t), BF16)


def _softmax_tile(s, m_ref, e, tile_bias):
    m_old = m_ref[e]
    m_blk = jnp.max(s, axis=0, keepdims=True)
    if tile_bias is not None:
        m_blk = m_blk + tile_bias
    m_new = jnp.maximum(m_old, m_blk)
    m_ref[e] = m_new
    shift = m_new if tile_bias is None else m_new - tile_bias
    return jnp.exp2(s - shift).astype(BF16), jnp.exp2(m_old - m_new)


def _accumulate(vt_blk, p, alpha, acc_ref, e):
    pv = jnp.dot(vt_blk, p, preferred_element_type=F32)
    acc_ref[e] = alpha * acc_ref[e] + pv


def _pipelined_heads(n, logits, softmax, accumulate):
    ahead = [logits(e) for e in range(min(LOGITS_AHEAD, n))]
    pending = []
    for e in range(n):
        s = ahead.pop(0)
        if e + LOGITS_AHEAD < n:
            ahead.append(logits(e + LOGITS_AHEAD))
        p, alpha = softmax(e, s)
        pending.append((e, p, alpha))
        if len(pending) > ACCUMULATE_BEHIND:
            accumulate(*pending.pop(0))
    for item in pending:
        accumulate(*item)


def _sweep_key_tiles(i, step, n=KEY_TILES_PER_TRIP):
    def body(trip, carry):
        step([(n * trip + k, False) for k in range(n)])
        return carry

    lax.fori_loop(0, i // n, body, 0)
    for left in range(n):
        @pl.when(i % n == left)
        def _():
            step([(i - left + k, False) for k in range(left)] + [(i, True)])


def _causal_tile_mask(t):
    key = lax.broadcasted_iota(jnp.int32, (t, t), 0)
    query = lax.broadcasted_iota(jnp.int32, (t, t), 1)
    return key <= query


def _mla_pre_kernel(h_ref, g0_ref, win_ref, gq_ref, gkv_ref, wqa_ref, wqb_ref, wk_ref, wv_ref,
                    cq_ref, sq_ref, ck_ref, sk_ref, qt_ref, k_ref, vt_ref):
    hn = _rms(h_ref[0], g0_ref[...]).astype(BF16)
    c = jnp.dot(hn, win_ref[...], preferred_element_type=F32)
    cq = _rms(c[:, :MLA_Q_RANK], gq_ref[...]).astype(BF16)
    ckv = _rms(c[:, MLA_Q_RANK:MLA_Q_RANK + MLA_KV_RANK], gkv_ref[...]).astype(BF16)
    o = MLA_Q_RANK + MLA_KV_RANK
    k_rope = (c[:, o:o + HEAD_LANES] * ck_ref[...]
              + c[:, o + HEAD_LANES:o + 2 * HEAD_LANES] * sk_ref[...])
    ka = jnp.dot(ckv, wk_ref[...], preferred_element_type=F32)
    for hd in range(MLA_HEADS):
        d = slice(hd * HEAD_LANES, (hd + 1) * HEAD_LANES)
        k_ref[0, :, d] = (ka[:, d] + k_rope).astype(BF16)
    cq_t = cq_ref[...]
    sq_t = sq_ref[...]
    per = 4
    for c0 in range(0, MLA_HEADS, per):
        rows = slice(c0 * HEAD_LANES, (c0 + per) * HEAD_LANES)
        qa = _dot_nt(wqa_ref[rows, :], cq)
        qb = _dot_nt(wqb_ref[rows, :], cq)
        for e in range(per):
            a = slice(e * HEAD_LANES, (e + 1) * HEAD_LANES)
            qt_ref[0, c0 + e] = (qa[a] * cq_t + qb[a] * sq_t).astype(BF16)
    _store_vt(vt_ref, _dot_nt(wv_ref[...], ckv), MLA_HEADS, MLA_V)


def _mla_pre(h, g0, w, tabs):
    b, lp, d = h.shape
    t = ATTN_TILE
    nblk = lp // t
    tok = lambda n: pl.BlockSpec((1, t, n), lambda bi, i: (bi, i, 0))
    tab = pl.BlockSpec((t, HEAD_LANES), lambda bi, i: (i, 0))
    tab_t = pl.BlockSpec((HEAD_LANES, t), lambda bi, i: (0, i))
    return pl.pallas_call(
        _mla_pre_kernel,
        grid=(b, nblk),
        in_specs=[tok(d), _const_spec(g0.shape), _const_spec(w["win"].shape),
                  _const_spec(w["gq"].shape), _const_spec(w["gkv"].shape),
                  _const_spec(w["wqa"].shape), _const_spec(w["wqb"].shape),
                  _const_spec(w["wk"].shape), _const_spec(w["wv"].shape), tab_t, tab_t, tab, tab],
        out_specs=[pl.BlockSpec((1, MLA_HEADS, HEAD_LANES, t), lambda bi, i: (bi, 0, 0, i)),
                   tok(MLA_HEADS * HEAD_LANES),
                   pl.BlockSpec((1, MLA_HEADS, 1, MLA_VT_ROWS, t), lambda bi, i: (bi, 0, i, 0, 0))],
        out_shape=[jax.ShapeDtypeStruct((b, MLA_HEADS, HEAD_LANES, lp), BF16),
                   jax.ShapeDtypeStruct((b, lp, MLA_HEADS * HEAD_LANES), BF16),
                   jax.ShapeDtypeStruct((b, MLA_HEADS, nblk, MLA_VT_ROWS, t), BF16)],
        compiler_params=_params(("parallel", "parallel")),
        name="mla_pre",
    )(h, g0, w["win"], w["gq"], w["gkv"], w["wqa"], w["wqb"], w["wk"], w["wv"], *tabs)


def _mla_attn_kernel(qt_ref, k_ref, vt_ref, o_ref, m_ref, acc_ref):
    t = ATTN_TILE
    i = pl.program_id(2)
    m_ref[...] = jnp.full(m_ref.shape, NEG_BIG, F32)
    acc_ref[...] = jnp.zeros(acc_ref.shape, F32)

    def step(tiles):
        g = MLA_GROUP

        def logits(v):
            (j, masked), e = tiles[v // g], v % g
            k_blk = k_ref[0, pl.ds(pl.multiple_of(j * t, t), t), e * HEAD_LANES:(e + 1) * HEAD_LANES]
            s = jnp.dot(k_blk, qt_ref[0, e], preferred_element_type=F32)
            return jnp.where(_causal_tile_mask(t), s, NEG_BIG) if masked else s

        _pipelined_heads(
            g * len(tiles), logits,
            lambda v, s: _softmax_tile(s, m_ref, v % g, None),
            lambda v, p, alpha: _accumulate(vt_ref[0, v % g, tiles[v // g][0]], p, alpha, acc_ref, v % g))

    _sweep_key_tiles(i, step, MLA_KEY_TILES_PER_TRIP)
    for p in range(MLA_GROUP // 2):
        halves = [acc_ref[2 * p + e, pl.ds(0, MLA_V), :] / acc_ref[2 * p + e, pl.ds(MLA_V, 1), :]
                  for e in range(2)]
        o_ref[0, :, p * HEAD_LANES:(p + 1) * HEAD_LANES] = jnp.concatenate(halves, axis=0).T.astype(BF16)


def _mla_attn(qt, k, vt):
    b, lp, _ = k.shape
    t = ATTN_TILE
    g = MLA_GROUP
    nblk = lp // t
    return pl.pallas_call(
        _mla_attn_kernel,
        grid=(b, MLA_HEADS // g, nblk),
        in_specs=[pl.BlockSpec((1, g, HEAD_LANES, t), lambda bi, gi, i: (bi, gi, 0, i)),
                  pl.BlockSpec((1, lp, g * HEAD_LANES), lambda bi, gi, i: (bi, 0, gi)),
                  pl.BlockSpec((1, g, nblk, MLA_VT_ROWS, t), lambda bi, gi, i: (bi, gi, 0, 0, 0))],
        out_specs=pl.BlockSpec((1, t, g * MLA_V), lambda bi, gi, i: (bi, i, gi)),
        out_shape=jax.ShapeDtypeStruct((b, lp, MLA_HEADS * MLA_V), BF16),
        scratch_shapes=[pltpu.VMEM((g, 1, t), F32), pltpu.VMEM((g, MLA_VT_ROWS, t), F32)],
        compiler_params=_params(("parallel", "parallel", "arbitrary")),
        name="mla_attn",
    )(qt, k, vt)


def _sc_pre_kernel(h_ref, g_ref, w_ref, cw_ref, o_ref, xs_ref, *, tm):
    d = D_MODEL

    @pl.when(pl.program_id(1) == 0)
    def _():
        xs_ref[pl.ds(tm, HALO), :] = jnp.zeros((HALO, d), F32)

    hn = _rms(h_ref[0], g_ref[...]).astype(BF16)
    gate_c = jnp.dot(hn, w_ref[:, d:2 * d], preferred_element_type=F32)
    u = jnp.dot(hn, w_ref[:, 2 * d:], preferred_element_type=F32)
    z = gate_c * u
    xs_ref[pl.ds(0, HALO), :] = xs_ref[pl.ds(tm, HALO), :]
    xs_ref[pl.ds(HALO, tm), :] = z
    y = _causal_conv(xs_ref, z, cw_ref[...], tm)
    gate_b = jnp.dot(hn, w_ref[:, :d], preferred_element_type=F32)
    o_ref[0] = (gate_b * y).astype(BF16)


def _sc_pre(h, g, w, cw, tm):
    b, lp, d = h.shape
    tok = pl.BlockSpec((1, tm, d), lambda bi, i: (bi, i, 0))
    return pl.pallas_call(
        functools.partial(_sc_pre_kernel, tm=tm),
        grid=(b, lp // tm),
        in_specs=[tok, _const_spec(g.shape), _const_spec(w.shape), _const_spec(cw.shape)],
        out_specs=tok,
        out_shape=jax.ShapeDtypeStruct((b, lp, d), BF16),
        scratch_shapes=[pltpu.VMEM((tm + HALO, d), F32)],
        compiler_params=_params(("parallel", "arbitrary")),
        name="sc_pre",
    )(h, g, w, cw)


def _diff_pre_kernel(h_ref, g_ref, wqt_ref, wk_ref, wvt_ref, qt_ref, k_ref, vt_ref):
    hn = _rms(h_ref[0], g_ref[...]).astype(BF16)
    k_ref[0] = jnp.dot(hn, wk_ref[...], preferred_element_type=F32).astype(BF16)
    qt = _dot_nt(wqt_ref[...], hn) * (DIFF_HEAD_DIM ** -0.5 * LOG2E)
    for hd in range(DIFF_HEADS):
        qt_ref[0, hd] = qt[hd * HEAD_LANES:(hd + 1) * HEAD_LANES].astype(BF16)
    _store_vt(vt_ref, _dot_nt(wvt_ref[...], hn), DIFF_HEADS, 2 * DIFF_HEAD_DIM)


def _diff_pre(h, g, wqt, wk, wvt):
    b, lp, d = h.shape
    t = ATTN_TILE
    nblk = lp // t
    tok = pl.BlockSpec((1, t, d), lambda bi, i: (bi, i, 0))
    return pl.pallas_call(
        _diff_pre_kernel,
        grid=(b, nblk),
        in_specs=[tok, _const_spec(g.shape), _const_spec(wqt.shape), _const_spec(wk.shape),
                  _const_spec(wvt.shape)],
        out_specs=[pl.BlockSpec((1, DIFF_HEADS, HEAD_LANES, t), lambda bi, i: (bi, 0, 0, i)), tok,
                   pl.BlockSpec((1, DIFF_HEADS, 1, DIFF_VT_ROWS, t), lambda bi, i: (bi, 0, i, 0, 0))],
        out_shape=[jax.ShapeDtypeStruct((b, DIFF_HEADS, HEAD_LANES, lp), BF16),
                   jax.ShapeDtypeStruct((b, lp, d), BF16),
                   jax.ShapeDtypeStruct((b, DIFF_HEADS, nblk, DIFF_VT_ROWS, t), BF16)],
        compiler_params=_params(("parallel", "parallel")),
        name="diff_pre",
    )(h, g, wqt, wk, wvt)


def _diff_attn_kernel(lam_ref, slope_ref, gsub_ref, qt_ref, k_ref, vt_ref, o_ref,
                      qs_ref, bias_ref, m_ref, acc_ref, *, lambda_init):
    t = ATTN_TILE
    dv = 2 * DIFF_HEAD_DIM
    i = pl.program_id(2)
    first_map = lax.broadcasted_iota(jnp.int32, (HEAD_LANES, t), 0) < DIFF_HEAD_DIM
    key = lax.broadcasted_iota(jnp.int32, (t, t), 0)
    query = lax.broadcasted_iota(jnp.int32, (t, t), 1)
    in_tile_dist = (key - query).astype(F32)
    for e in range(DIFF_GROUP):
        q = qt_ref[0, e]
        zero = jnp.zeros_like(q)
        qs_ref[e, :, pl.ds(0, t)] = jnp.where(first_map, q, zero)
        qs_ref[e, :, pl.ds(t, t)] = jnp.where(first_map, zero, q)
        bias_ref[e] = in_tile_dist * slope_ref[e][:, 0:1]
    m_ref[...] = jnp.full(m_ref.shape, NEG_BIG, F32)
    acc_ref[...] = jnp.zeros(acc_ref.shape, F32)

    def step(tiles):
        g = DIFF_GROUP

        def logits(v):
            (j, masked), e = tiles[v // g], v % g
            k_blk = k_ref[0, pl.ds(pl.multiple_of(j * t, t), t), e * HEAD_LANES:(e + 1) * HEAD_LANES]
            s = jnp.dot(k_blk, qs_ref[e], preferred_element_type=F32)
            bias = bias_ref[e]
            maps = [s[:, :t] + bias, s[:, t:] + bias]
            if masked:
                maps = [jnp.where(key <= query, sm, NEG_BIG) for sm in maps]
            return jnp.concatenate(maps, axis=1)

        def softmax(v, s):
            j, e = tiles[v // g][0], v % g
            tile_bias = slope_ref[e][:, 0:1] * ((j - i) * t).astype(F32)
            return _softmax_tile(s, m_ref, e, tile_bias)

        _pipelined_heads(
            g * len(tiles), logits, softmax,
            lambda v, p, alpha: _accumulate(vt_ref[0, v % g, tiles[v // g][0]], p, alpha, acc_ref, v % g))

    _sweep_key_tiles(i, step)

    lv = lam_ref[...]
    lam = (jnp.exp(jnp.sum(lv[0:1] * lv[1:2], axis=-1, keepdims=True))
           - jnp.exp(jnp.sum(lv[2:3] * lv[3:4], axis=-1, keepdims=True)) + lambda_init)
    for e in range(DIFF_GROUP):
        o_t = acc_ref[e, pl.ds(0, dv), :] / acc_ref[e, pl.ds(dv, 1), :]
        o = (o_t[:, :t] - lam * o_t[:, t:]).T
        o_ref[0, :, e * dv:(e + 1) * dv] = (_rms(o, gsub_ref[...]) * (1.0 - lambda_init)).astype(BF16)


def _diff_attn(qt, k, vt, lam_vecs, slopes, gsub, lambda_init):
    b, lp, _ = k.shape
    t = ATTN_TILE
    g = DIFF_GROUP
    nblk = lp // t
    return pl.pallas_call(
        functools.partial(_diff_attn_kernel, lambda_init=lambda_init),
        grid=(b, DIFF_HEADS // g, nblk),
        in_specs=[_const_spec(lam_vecs.shape),
                  pl.BlockSpec((g, 1, HEAD_LANES), lambda bi, gi, i: (gi, 0, 0)),
                  _const_spec(gsub.shape),
                  pl.BlockSpec((1, g, HEAD_LANES, t), lambda bi, gi, i: (bi, gi, 0, i)),
                  pl.BlockSpec((1, lp, g * HEAD_LANES), lambda bi, gi, i: (bi, 0, gi)),
                  pl.BlockSpec((1, g, nblk, DIFF_VT_ROWS, t), lambda bi, gi, i: (bi, gi, 0, 0, 0))],
        out_specs=pl.BlockSpec((1, t, g * HEAD_LANES), lambda bi, gi, i: (bi, i, gi)),
        out_shape=jax.ShapeDtypeStruct((b, lp, DIFF_HEADS * HEAD_LANES), BF16),
        scratch_shapes=[pltpu.VMEM((g, HEAD_LANES, 2 * t), BF16), pltpu.VMEM((g, t, t), F32),
                        pltpu.VMEM((g, 1, 2 * t), F32), pltpu.VMEM((g, DIFF_VT_ROWS, 2 * t), F32)],
        compiler_params=_params(("parallel", "parallel", "arbitrary")),
        name="diff_attn",
    )(lam_vecs, slopes, gsub, qt, k, vt)


SUBLANES = 8
LANES = 128
FFN_HALO = 2 * SUBLANES


def _interleaved_rows(tm):
    return tm // SUBLANES


def _ffn_chunks():
    return [(lo, min(FFN_CHUNK, D_FF - lo)) for lo in range(0, D_FF, FFN_CHUNK)]


def _store_lane_tiles(ref, x):
    for l in range(ref.shape[0]):
        ref[l] = x[:, l * LANES:(l + 1) * LANES]


def _strided_rows(ref, start, n, stride):
    return jnp.concatenate([ref[l, pl.ds(start, n, stride=stride), :] for l in range(ref.shape[0])], axis=1)


def _post_kernel(h_ref, o_ref, wo_ref, g_ref, wup_ref, cw_ref, wdn_ref, out_ref,
                 nat_ref, hn_ref, xs_ref, carry_ref, f_ref, *, tm):
    chunks = _ffn_chunks()
    run = _interleaved_rows(tm)

    @pl.when(pl.program_id(1) == 0)
    def _():
        carry_ref[...] = jnp.zeros(carry_ref.shape, F32)

    g = g_ref[...]
    m = jnp.dot(o_ref[0], wo_ref[...], preferred_element_type=F32)
    h1 = h_ref[0] + _rms(m, g[1:2])
    out_ref[0] = h1
    _store_lane_tiles(nat_ref, _rms(h1, g[2:3]))
    for r in range(0, run, 2):
        pair = [_strided_rows(nat_ref, r + k, SUBLANES, run) for k in range(2)]
        hn_ref[pl.ds(r * SUBLANES, 2 * SUBLANES), :] = jnp.concatenate(pair, axis=0).astype(BF16)
    f_ref[...] = jnp.zeros(f_ref.shape, F32)

    def up_proj(c):
        lo, w = chunks[c]
        hn = hn_ref[...]
        gate = jnp.dot(hn, wup_ref[:, lo:lo + w], preferred_element_type=F32)
        value = jnp.dot(hn, wup_ref[:, D_FF + lo:D_FF + lo + w], preferred_element_type=F32)
        return gate, value

    def conv_act(c, up):
        lo, w = chunks[c]
        xs = xs_ref.at[c % 2]
        cols = pl.ds(0, 2 * w)
        xs[pl.ds(FFN_HALO, tm), pl.ds(0, w)] = up[0]
        xs[pl.ds(FFN_HALO, tm), pl.ds(w, w)] = up[1]
        first_sublane = lax.broadcasted_iota(jnp.int32, (SUBLANES, 2 * w), 0) == 0
        for k in range(2):
            cur = pltpu.roll(xs[pl.ds(tm + k * SUBLANES, SUBLANES), cols], 1, 0)
            prev = pltpu.roll(carry_ref[c, pl.ds(k * SUBLANES, SUBLANES), cols], 1, 0)
            xs[pl.ds(k * SUBLANES, SUBLANES), cols] = jnp.where(first_sublane, prev, cur)
        carry_ref[c, :, cols] = xs[pl.ds(tm, FFN_HALO), cols]
        cw = jnp.concatenate([cw_ref[:, lo:lo + w], cw_ref[:, D_FF + lo:D_FF + lo + w]], axis=1)
        y = (cw[2:3] * xs[pl.ds(FFN_HALO, tm), cols] + cw[1:2] * xs[pl.ds(SUBLANES, tm), cols]
             + cw[0:1] * xs[pl.ds(0, tm), cols])
        half_gate = 0.5 * y[:, :w]
        silu = half_gate + half_gate * jnp.tanh(half_gate)
        return (silu * y[:, w:]).astype(BF16)

    nch = len(chunks)
    up_next = up_proj(0)
    for c in range(nch):
        up = up_next
        if c + 1 < nch:
            up_next = up_proj(c + 1)
        act = conv_act(c, up)
        lo, w = chunks[c]
        f_ref[...] += jnp.dot(act, wdn_ref[lo:lo + w, :], preferred_element_type=F32)
    _store_lane_tiles(nat_ref, _rms(f_ref[...], g[3:4]))
    for j in range(tm // SUBLANES):
        pieces, t = [], j * SUBLANES
        while t < (j + 1) * SUBLANES:
            s_, r_ = divmod(t, run)
            n = min((j + 1) * SUBLANES - t, run - r_)
            pieces.append(_strided_rows(nat_ref, r_ * SUBLANES + s_, n, SUBLANES))
            t += n
        rows = pl.ds(j * SUBLANES, SUBLANES)
        out_ref[0, rows, :] = out_ref[0, rows, :] + (pieces[0] if len(pieces) == 1 else jnp.concatenate(pieces, axis=0))


def _post(h, o, wo, g, wup, cw, wdn, tm):
    b, lp, d = h.shape
    nch = len(_ffn_chunks())
    tok = pl.BlockSpec((1, tm, d), lambda bi, i: (bi, i, 0))
    return pl.pallas_call(
        functools.partial(_post_kernel, tm=tm),
        grid=(b, lp // tm),
        in_specs=[tok, tok, _const_spec(wo.shape), _const_spec(g.shape), _const_spec(wup.shape),
                  _const_spec(cw.shape), _const_spec(wdn.shape)],
        out_specs=tok,
        out_shape=jax.ShapeDtypeStruct((b, lp, d), F32),
        scratch_shapes=[pltpu.VMEM((d // LANES, tm, LANES), F32), pltpu.VMEM((tm, d), BF16),
                        pltpu.VMEM((2, tm + FFN_HALO, 2 * FFN_CHUNK), F32),
                        pltpu.VMEM((nch, FFN_HALO, 2 * FFN_CHUNK), F32), pltpu.VMEM((tm, d), F32)],
        compiler_params=_params(("parallel", "arbitrary")),
        name="post_ffn",
    )(h, o, wo, g, wup, cw, wdn)


def _mla_weights(w_in, g_q, g_kv, w_uq, w_ukv):
    qr, kvr, r2 = MLA_Q_RANK, MLA_KV_RANK, MLA_ROPE // 2
    kr = w_in[:, qr + kvr:]
    pad = lambda a, lo: jnp.pad(a, ((0, 0), (lo, HEAD_LANES - lo - a.shape[1])))
    kr_swapped = jnp.concatenate([kr[:, r2:], kr[:, :r2]], axis=1)
    win = jnp.concatenate([w_in[:, :qr + kvr], pad(kr, MLA_NOPE), pad(kr_swapped, MLA_NOPE)], axis=1)
    wq = w_uq.reshape(qr, MLA_HEADS, MLA_QK)
    wqa = jnp.pad(wq, ((0, 0), (0, 0), (0, HEAD_LANES - MLA_QK)))
    rope_swapped = jnp.concatenate([wq[..., MLA_NOPE + r2:], wq[..., MLA_NOPE:MLA_NOPE + r2]], axis=-1)
    wqb = jnp.pad(rope_swapped, ((0, 0), (0, 0), (MLA_NOPE, HEAD_LANES - MLA_QK)))
    wkv = w_ukv.reshape(kvr, MLA_HEADS, MLA_NOPE + MLA_V)
    wk = jnp.pad(wkv[..., :MLA_NOPE], ((0, 0), (0, 0), (0, HEAD_LANES - MLA_NOPE)))
    wv = wkv[..., MLA_NOPE:]
    flat = lambda a: a.reshape(a.shape[0], -1).astype(BF16)
    return dict(win=win.astype(BF16), gq=g_q[None], gkv=g_kv[None], wqa=flat(wqa).T, wqb=flat(wqb).T,
                wk=flat(wk), wv=flat(wv).T)


def _rope_tables(lp):
    inv_freq = ROPE_THETA ** (-jnp.arange(0, MLA_ROPE, 2, dtype=F32) / MLA_ROPE)
    ang = jnp.arange(lp, dtype=F32)[:, None] * inv_freq[None, :]
    cos, sin = jnp.cos(ang), jnp.sin(ang)
    lay = lambda nope, a, b_: jnp.concatenate(
        [jnp.full((lp, MLA_NOPE), nope, F32), a, b_, jnp.zeros((lp, HEAD_LANES - MLA_QK), F32)], axis=1)
    scale = MLA_QK ** -0.5 * LOG2E
    return ((lay(1.0, cos, cos) * scale).T, (lay(0.0, -sin, sin) * scale).T,
            lay(0.0, cos, cos), lay(0.0, -sin, sin))


def kernel(x, meta_tokens, norms, mla_w_in, mla_norm_q, mla_norm_kv, mla_w_uq, mla_w_ukv, mla_w_o, sc_w_in, sc_conv, sc_w_out, diff_w_in, diff_lambda_q1, diff_lambda_k1, diff_lambda_q2, diff_lambda_k2, diff_subln, diff_w_o, ffn_w_up, ffn_conv, ffn_w_down):
    b, seq, d = x.shape
    depth = norms.shape[0]
    length = N_META + seq
    lp = -(-length // ATTN_TILE) * ATTN_TILE
    tm = _token_tile(lp)
    meta = jnp.broadcast_to(meta_tokens[None].astype(x.dtype), (b, N_META, d))
    h = jnp.concatenate([meta, x, jnp.zeros((b, lp - length, d), x.dtype)], axis=1)
    tabs = _rope_tables(lp)
    slopes = 2.0 ** (-8.0 * jnp.arange(1, DIFF_HEADS + 1, dtype=F32) / DIFF_HEADS) * LOG2E
    slopes = jnp.broadcast_to(slopes[:, None, None], (DIFF_HEADS, 1, HEAD_LANES))

    for i in range(depth):
        kind, j = i % N_MIXERS, i // N_MIXERS
        g = norms[i]
        if kind == 0:
            w = _mla_weights(mla_w_in[j], mla_norm_q[j], mla_norm_kv[j], mla_w_uq[j], mla_w_ukv[j])
            o = _mla_attn(*_mla_pre(h, g[0:1], w, tabs))
            wo = mla_w_o[j]
        elif kind == 1:
            o = _sc_pre(h, g[0:1], sc_w_in[j].astype(BF16), sc_conv[j], tm)
            wo = sc_w_out[j]
        else:
            lambda_init = 0.8 - 0.6 * math.exp(-0.3 * i)
            w_in = diff_w_in[j].astype(BF16)
            qt, k, vt = _diff_pre(h, g[0:1], w_in[:, :d].T, w_in[:, d:2 * d], w_in[:, 2 * d:].T)
            lam_vecs = jnp.stack([diff_lambda_q1[j], diff_lambda_k1[j], diff_lambda_q2[j], diff_lambda_k2[j]])
            o = _diff_attn(qt, k, vt, lam_vecs, slopes, diff_subln[j][None], lambda_init)
            wo = diff_w_o[j]
        h = _post(h, o, wo.astype(BF16), g, ffn_w_up[i].astype(BF16), ffn_conv[i],
                  ffn_w_down[i].astype(BF16), tm)
    return h[:, N_META:length]
```
